```python
import jax, jax.numpy as jnp
from jax import lax
import numpy as np

D_MODEL = 1024
BATCH = 2
SEQ = 8192
DEPTH = 2

N_META = 16
BLOCK = 128
ROPE_THETA = 10000.0
NORM_EPS = 1e-6
NEG_INF = -1e30

A_HEADS = 16
A_KV_HEADS = 2
A_HEAD_DIM = D_MODEL // A_HEADS
A_GROUP = A_HEADS // A_KV_HEADS
A_WINDOW = 128
A_Q_W = A_HEADS * A_HEAD_DIM
A_KV_W = A_KV_HEADS * A_HEAD_DIM
A_QKV = A_Q_W + 2 * A_KV_W

B_HEADS = 16
B_NOPE = 64
B_ROPE = 32
B_V = 64
B_Q_RANK = D_MODEL // 4
B_KV_RANK = D_MODEL // 8
B_IN = B_Q_RANK + B_KV_RANK + B_ROPE

N_GROUPS = 4
EXPERTS_PER_GROUP = 8
N_EXPERTS = N_GROUPS * EXPERTS_PER_GROUP
TOP_K = 2
D_EXPERT = D_MODEL // 4
MOE_BLOCK = 128

N_A_LAYERS = (DEPTH + 1) // 2
N_B_LAYERS = DEPTH // 2

kernel_name = "hybrid_swa_sink_mla_hier_moe_meta"


def rms_norm(x, g):
    xf = x.astype(jnp.float32)
    y = xf * lax.rsqrt(jnp.mean(xf * xf, axis=-1, keepdims=True) + NORM_EPS)
    return (y * g.astype(jnp.float32)).astype(x.dtype)


def rope_tables(length, dim):
    inv_freq = 1.0 / (ROPE_THETA ** (jnp.arange(0, dim, 2, dtype=jnp.float32) / dim))
    ang = jnp.arange(length, dtype=jnp.float32)[:, None] * inv_freq[None, :]
    return jnp.cos(ang), jnp.sin(ang)


def apply_rope(x, cos, sin):
    half = x.shape[-1] // 2
    xf = x.astype(jnp.float32)
    x1, x2 = xf[..., :half], xf[..., half:]
    return jnp.concatenate([x1 * cos - x2 * sin, x2 * cos + x1 * sin], axis=-1).astype(x.dtype)


def sink_softmax(s, sink):
    m = jnp.maximum(jnp.max(s, axis=-1, keepdims=True), sink)
    p = jnp.exp(s - m)
    return p / (jnp.sum(p, axis=-1, keepdims=True) + jnp.exp(sink - m))


def swa_sink_attention(h, w_qkv, b_qkv, sinks, w_o, b_o):
    B, L, _ = h.shape
    n_blk = (L - N_META) // BLOCK
    qkv = h @ w_qkv + b_qkv
    q = qkv[..., :A_Q_W].reshape(B, L, A_KV_HEADS, A_GROUP, A_HEAD_DIM)
    k = qkv[..., A_Q_W:A_Q_W + A_KV_W].reshape(B, L, A_KV_HEADS, A_HEAD_DIM)
    v = qkv[..., A_Q_W + A_KV_W:].reshape(B, L, A_KV_HEADS, A_HEAD_DIM)
    cos, sin = rope_tables(L, A_HEAD_DIM)
    q = apply_rope(q, cos[:, None, None], sin[:, None, None]) * (A_HEAD_DIM ** -0.5)
    k = apply_rope(k, cos[:, None], sin[:, None])
    sink = sinks.astype(jnp.float32).reshape(A_KV_HEADS, A_GROUP, 1, 1)

    qm, km, vm = q[:, :N_META], k[:, :N_META], v[:, :N_META]
    causal_m = jnp.tril(jnp.ones((N_META, N_META), dtype=bool))
    s_m = jnp.einsum('bqkgd,bskd->bkgqs', qm, km).astype(jnp.float32)
    p_m = sink_softmax(jnp.where(causal_m, s_m, NEG_INF), sink).astype(v.dtype)
    o_m = jnp.einsum('bkgqs,bskd->bqkgd', p_m, vm).reshape(B, N_META, A_Q_W)

    qr = q[:, N_META:].reshape(B, n_blk, BLOCK, A_KV_HEADS, A_GROUP, A_HEAD_DIM)
    kr = k[:, N_META:].reshape(B, n_blk, BLOCK, A_KV_HEADS, A_HEAD_DIM)
    vr = v[:, N_META:].reshape(B, n_blk, BLOCK, A_KV_HEADS, A_HEAD_DIM)

    def band(t, t_meta):
        prev = jnp.concatenate([jnp.zeros_like(t[:, :1]), t[:, :-1]], axis=1)
        meta = jnp.broadcast_to(t_meta[:, None], (B, n_blk) + t_meta.shape[1:])
        return jnp.concatenate([meta, prev, t], axis=2)

    kb, vb = band(kr, km), band(vr, vm)
    i = jnp.arange(BLOCK)[:, None]
    j = jnp.arange(BLOCK)[None, :]
    blk = jnp.arange(n_blk)[:, None, None]
    prev_ok = ((BLOCK + i - j) < A_WINDOW)[None] & (blk > 0)
    cur_ok = ((j <= i) & ((i - j) < A_WINDOW))[None]
    mask = jnp.concatenate([
        jnp.ones((n_blk, BLOCK, N_META), dtype=bool),
        jnp.broadcast_to(prev_ok, (n_blk, BLOCK, BLOCK)),
        jnp.broadcast_to(cur_ok, (n_blk, BLOCK, BLOCK))], axis=-1)
    s = jnp.einsum('bnqkgd,bnskd->bnkgqs', qr, kb).astype(jnp.float32)
    p = sink_softmax(jnp.where(mask[:, None, None], s, NEG_INF), sink).astype(v.dtype)
    o_r = jnp.einsum('bnkgqs,bnskd->bnqkgd', p, vb).reshape(B, n_blk * BLOCK, A_Q_W)

    o = jnp.concatenate([o_m, o_r], axis=1)
    return o @ w_o + b_o


def mla_attention(h, w_in, q_norm, kv_norm, w_uq, w_ukv, w_o):
    B, L, _ = h.shape
    n_blk = (L - N_META) // BLOCK
    c = h @ w_in
    c_q = c[..., :B_Q_RANK]
    c_kv = c[..., B_Q_RANK:B_Q_RANK + B_KV_RANK]
    k_pe = c[..., B_Q_RANK + B_KV_RANK:]
    q = (rms_norm(c_q, q_norm) @ w_uq).reshape(B, L, B_HEADS, B_NOPE + B_ROPE)
    kv = (rms_norm(c_kv, kv_norm) @ w_ukv).reshape(B, L, B_HEADS, B_NOPE + B_V)
    q_nope, q_pe = q[..., :B_NOPE], q[..., B_NOPE:]
    k_nope, v = kv[..., :B_NOPE], kv[..., B_NOPE:]
    cos, sin = rope_tables(L, B_ROPE)
    scale = (B_NOPE + B_ROPE) ** -0.5
    q_nope = q_nope * scale
    q_pe = apply_rope(q_pe, cos[:, None], sin[:, None]) * scale
    k_pe = apply_rope(k_pe, cos, sin)

    causal_m = jnp.tril(jnp.ones((N_META, N_META), dtype=bool))
    s_m = (jnp.einsum('bqhd,bshd->bhqs', q_nope[:, :N_META], k_nope[:, :N_META])
           + jnp.einsum('bqhd,bsd->bhqs', q_pe[:, :N_META], k_pe[:, :N_META])).astype(jnp.float32)
    p_m = jax.nn.softmax(jnp.where(causal_m, s_m, NEG_INF), axis=-1).astype(v.dtype)
    o_m = jnp.einsum('bhqs,bshd->bqhd', p_m, v[:, :N_META]).reshape(B, N_META, B_HEADS * B_V)

    qn_blocks = q_nope[:, N_META:].reshape(B, n_blk, BLOCK, B_HEADS, B_NOPE).transpose(1, 0, 2, 3, 4)
    qp_blocks = q_pe[:, N_META:].reshape(B, n_blk, BLOCK, B_HEADS, B_ROPE).transpose(1, 0, 2, 3, 4)
    k_pos = jnp.arange(L)

    def attend_block(args):
        qn, qp, b = args
        q_pos = N_META + b * BLOCK + jnp.arange(BLOCK)
        s = (jnp.einsum('bqhd,bshd->bhqs', qn, k_nope)
             + jnp.einsum('bqhd,bsd->bhqs', qp, k_pe)).astype(jnp.float32)
        s = jnp.where(k_pos[None, :] <= q_pos[:, None], s, NEG_INF)
        p = jax.nn.softmax(s, axis=-1).astype(v.dtype)
        return jnp.einsum('bhqs,bshd->bqhd', p, v)

    o_r = lax.map(attend_block, (qn_blocks, qp_blocks, jnp.arange(n_blk)))
    o_r = o_r.transpose(1, 0, 2, 3, 4).reshape(B, n_blk * BLOCK, B_HEADS * B_V)
    o = jnp.concatenate([o_m, o_r], axis=1)
    return o @ w_o


def hierarchical_moe(h, w_group, b_group, w_router, b_router, w_gate, w_up, w_down):
    B, L, D = h.shape
    T = B * L
    x = h.reshape(T, D)
    g_prob = jax.nn.softmax((x @ w_group).astype(jnp.float32) + b_group.astype(jnp.float32), axis=-1)
    g_p, g_idx = lax.top_k(g_prob, 1)
    e_logits = ((x @ w_router).astype(jnp.float32) + b_router.astype(jnp.float32)).reshape(T, N_GROUPS, EXPERTS_PER_GROUP)
    e_logits = jnp.take_along_axis(e_logits, g_idx[:, :, None], axis=1)[:, 0]
    e_prob = jax.nn.softmax(e_logits, axis=-1)
    e_p, e_idx = lax.top_k(e_prob, TOP_K)
    weights = g_p * e_p / jnp.sum(e_p, axis=-1, keepdims=True)
    expert_ids = g_idx * EXPERTS_PER_GROUP + e_idx

    n_slots = T * TOP_K
    slot_expert = expert_ids.reshape(-1)
    slot_token = jnp.repeat(jnp.arange(T), TOP_K)
    slot_weight = weights.reshape(-1)
    order = jnp.argsort(slot_expert)
    se, st, sw = slot_expert[order], slot_token[order], slot_weight[order]
    counts = jnp.bincount(slot_expert, length=N_EXPERTS)
    starts = jnp.cumsum(counts) - counts
    padded = (counts + MOE_BLOCK - 1) // MOE_BLOCK * MOE_BLOCK
    pends = jnp.cumsum(padded)
    pstarts = pends - padded
    dest = pstarts[se] + jnp.arange(n_slots) - starts[se]
    n_blocks = -(-(n_slots + N_EXPERTS * (MOE_BLOCK - 1)) // MOE_BLOCK)
    buf = jnp.zeros((n_blocks * MOE_BLOCK, D), x.dtype).at[dest].set(x[st])
    block_expert = jnp.minimum(jnp.searchsorted(pends, jnp.arange(n_blocks) * MOE_BLOCK, side='right'), N_EXPERTS - 1)

    def expert_block(args):
        xb, e = args
        return (jax.nn.silu(xb @ w_gate[e]) * (xb @ w_up[e])) @ w_down[e]

    y_buf = lax.map(expert_block, (buf.reshape(n_blocks, MOE_BLOCK, D), block_expert)).reshape(-1, D)
    y = jax.ops.segment_sum(y_buf[dest] * sw[:, None].astype(y_buf.dtype), st, num_segments=T)
    return y.reshape(B, L, D)


def setup_inputs(seed: int = 0) -> dict:
    key = jax.random.key(seed)
    ks = iter(jax.random.split(key, 32))
    f32 = jnp.float32

    def nrm(shape):
        return jax.random.normal(next(ks), shape, f32)

    def dense(shape, fan_in):
        return nrm(shape) * (fan_in ** -0.5)

    def gain(shape):
        return 1.0 + 0.02 * nrm(shape)

    return {
        "x": nrm((BATCH, SEQ, D_MODEL)),
        "meta_tokens": nrm((N_META, D_MODEL)),
        "attn_norm": gain((DEPTH, D_MODEL)),
        "ffn_norm": gain((DEPTH, D_MODEL)),
        "final_norm": gain((D_MODEL,)),
        "a_w_qkv": dense((N_A_LAYERS, D_MODEL, A_QKV), D_MODEL),
        "a_b_qkv": 0.02 * nrm((N_A_LAYERS, A_QKV)),
        "a_sinks": 0.5 * nrm((N_A_LAYERS, A_HEADS)),
        "a_w_o": dense((N_A_LAYERS, A_Q_W, D_MODEL), A_Q_W),
        "a_b_o": 0.02 * nrm((N_A_LAYERS, D_MODEL)),
        "b_w_in": dense((N_B_LAYERS, D_MODEL, B_IN), D_MODEL),
        "b_q_norm": gain((N_B_LAYERS, B_Q_RANK)),
        "b_kv_norm": gain((N_B_LAYERS, B_KV_RANK)),
        "b_w_uq": dense((N_B_LAYERS, B_Q_RANK, B_HEADS * (B_NOPE + B_ROPE)), B_Q_RANK),
        "b_w_ukv": dense((N_B_LAYERS, B_KV_RANK, B_HEADS * (B_NOPE + B_V)), B_KV_RANK),
        "b_w_o": dense((N_B_LAYERS, B_HEADS * B_V, D_MODEL), B_HEADS * B_V),
        "moe_w_group": dense((DEPTH, D_MODEL, N_GROUPS), D_MODEL),
        "moe_b_group": 0.01 * nrm((DEPTH, N_GROUPS)),
        "moe_w_router": dense((DEPTH, D_MODEL, N_EXPERTS), D_MODEL),
        "moe_b_router": 0.01 * nrm((DEPTH, N_EXPERTS)),
        "moe_w_gate": dense((DEPTH, N_EXPERTS, D_MODEL, D_EXPERT), D_MODEL),
        "moe_w_up": dense((DEPTH, N_EXPERTS, D_MODEL, D_EXPERT), D_MODEL),
        "moe_w_down": dense((DEPTH, N_EXPERTS, D_EXPERT, D_MODEL), D_EXPERT),
    }


def reference(x, meta_tokens, attn_norm, ffn_norm, final_norm,
              a_w_qkv, a_b_qkv, a_sinks, a_w_o, a_b_o,
              b_w_in, b_q_norm, b_kv_norm, b_w_uq, b_w_ukv, b_w_o,
              moe_w_group, moe_b_group, moe_w_router, moe_b_router,
              moe_w_gate, moe_w_up, moe_w_down):
    B = x.shape[0]
    meta = jnp.broadcast_to(meta_tokens[None].astype(x.dtype), (B, N_META, D_MODEL))
    h = jnp.concatenate([meta, x], axis=1)
    for i in range(DEPTH):
        hn = rms_norm(h, attn_norm[i])
        if i % 2 == 0:
            j = i // 2
            mix = swa_sink_attention(hn, a_w_qkv[j], a_b_qkv[j], a_sinks[j], a_w_o[j], a_b_o[j])
        else:
            j = i // 2
            mix = mla_attention(hn, b_w_in[j], b_q_norm[j], b_kv_norm[j], b_w_uq[j], b_w_ukv[j], b_w_o[j])
        h = h + mix
        h = h + hierarchical_moe(rms_norm(h, ffn_norm[i]), moe_w_group[i], moe_b_group[i],
                                 moe_w_router[i], moe_b_router[i],
                                 moe_w_gate[i], moe_w_up[i], moe_w_down[i])
    return rms_norm(h, final_norm)[:, N_META:]
```

```python
import functools

import jax
import jax.numpy as jnp
from jax import lax
from jax.experimental import pallas as pl
from jax.experimental.pallas import tpu as pltpu

F32 = jnp.float32
BF16 = jnp.bfloat16

D_MODEL = 1024
N_META = 16
BLOCK = 128
ROPE_THETA = 10000.0
NORM_EPS = 1e-6
NEG_INF = -1e30

A_HEADS = 16
A_KV_HEADS = 2
A_HEAD_DIM = 64
A_Q_W = A_HEADS * A_HEAD_DIM
A_KV_W = A_KV_HEADS * A_HEAD_DIM

B_HEADS = 16
B_NOPE = 64
B_ROPE = 32
B_V = 64
B_Q_RANK = 256
B_KV_RANK = 128
B_HEAD_PAD = 128

N_GROUPS = 4
EXPERTS_PER_GROUP = 8
N_EXPERTS = N_GROUPS * EXPERTS_PER_GROUP
TOP_K = 2
D_EXPERT = 256
MOE_BLOCK = 128
ROUTE_LANES = 128

ROW_TILE = 640
VMEM_LIMIT = 56 * 1024 * 1024


def _rms(x, g):
    return x * lax.rsqrt(jnp.mean(x * x, axis=-1, keepdims=True) + NORM_EPS) * g


def _params(*sem):
    return pltpu.CompilerParams(dimension_semantics=sem, vmem_limit_bytes=VMEM_LIMIT)


def _swa_qkv_kernel(h_ref, g_ref, w_ref, b_ref, cos_ref, sin_ref, q_ref, k_ref, v_ref):
    xn = _rms(h_ref[...], g_ref[...]).astype(BF16)
    y = jnp.dot(xn, w_ref[...], preferred_element_type=F32) + b_ref[...]
    cos = cos_ref[...]
    sin = sin_ref[...]
    lane = lax.broadcasted_iota(jnp.int32, cos.shape, 1)
    first = (lane & (A_HEAD_DIM - 1)) < (A_HEAD_DIM // 2)

    def rope(c):
        rot = jnp.where(first, pltpu.roll(c, 128 - A_HEAD_DIM // 2, 1), pltpu.roll(c, A_HEAD_DIM // 2, 1))
        return c * cos + rot * sin

    for j in range(A_Q_W // 128):
        q_ref[:, j * 128:(j + 1) * 128] = (rope(y[:, j * 128:(j + 1) * 128]) * (A_HEAD_DIM ** -0.5)).astype(BF16)
    k_ref[:, 0:128] = rope(y[:, A_Q_W:A_Q_W + 128]).astype(BF16)
    k_ref[:, 128:256] = rope(y[:, A_Q_W + 128:A_Q_W + 256]).astype(BF16)
    v_ref[...] = y[:, A_Q_W + 256:A_Q_W + 512].astype(BF16)


def _swa_qkv(h, g, w, b, cos, sin, lp):
    tp = h.shape[0]
    tiles_per_batch = lp // ROW_TILE
    n_out = w.shape[1]
    row = lambda i: (i, 0)
    fixed = lambda i: (0, 0)
    tab = lambda i: (i % tiles_per_batch, 0)
    return pl.pallas_call(
        _swa_qkv_kernel,
        grid=(tp // ROW_TILE,),
        in_specs=[
            pl.BlockSpec((ROW_TILE, D_MODEL), row),
            pl.BlockSpec((1, D_MODEL), fixed),
            pl.BlockSpec((D_MODEL, n_out), fixed),
            pl.BlockSpec((1, n_out), fixed),
            pl.BlockSpec((ROW_TILE, 128), tab),
            pl.BlockSpec((ROW_TILE, 128), tab),
        ],
        out_specs=[
            pl.BlockSpec((ROW_TILE, A_Q_W), row),
            pl.BlockSpec((ROW_TILE, 256), row),
            pl.BlockSpec((ROW_TILE, 256), row),
        ],
        out_shape=[
            jax.ShapeDtypeStruct((tp, A_Q_W), BF16),
            jax.ShapeDtypeStruct((tp, 256), BF16),
            jax.ShapeDtypeStruct((tp, 256), BF16),
        ],
        compiler_params=_params("parallel"),
        name="swa_qkv",
    )(h, g, w, b, cos, sin)


def _swa_attn_kernel(q_ref, kc_ref, kp_ref, km_ref, vc_ref, vp_ref, vm_ref, sink_ref, o_ref, *, n_real_blocks):
    n = pl.program_id(1)
    is_real = n < n_real_blocks
    n_keys = N_META + 2 * BLOCK
    big = jnp.int32(1 << 20)
    row = lax.broadcasted_iota(jnp.int32, (BLOCK, n_keys), 0)
    col = lax.broadcasted_iota(jnp.int32, (BLOCK, n_keys), 1)
    jp = col - N_META
    jc = col - (N_META + BLOCK)
    meta_lim = jnp.where(is_real, N_META, 0)
    prev_off = jnp.where(jnp.logical_and(is_real, n >= 1), 0, big)
    cur_lim = jnp.where(is_real, big, N_META)
    ok = (col < meta_lim) | ((jc < 0) & (jp > row + prev_off)) | ((jc >= 0) & (jc <= row) & (jc < cur_lim))
    bias1 = jnp.where(ok, 0.0, NEG_INF).astype(F32)
    bias = jnp.concatenate([bias1] * 4, axis=0)

    kcat = jnp.concatenate([km_ref[0], kp_ref[0], kc_ref[0]], axis=0)
    vcat = jnp.concatenate([vm_ref[0], vp_ref[0], vc_ref[0]], axis=0)
    lo = lax.broadcasted_iota(jnp.int32, (n_keys, 128), 1) < A_HEAD_DIM
    zero = jnp.zeros((n_keys, 128), BF16)
    q = q_ref[0]

    for kvh in range(A_KV_HEADS):
        a, b = (kcat[:, :128], kcat[:, 128:]) if kvh == 0 else (kcat[:, 128:], kcat[:, :128])
        k_lo = jnp.where(lo, a, zero)
        k_hi = jnp.where(lo, zero, b)
        a, b = (vcat[:, :128], vcat[:, 128:]) if kvh == 0 else (vcat[:, 128:], vcat[:, :128])
        v_lo = jnp.where(lo, a, zero)
        v_hi = jnp.where(lo, zero, b)
        base = kvh * (A_Q_W // A_KV_HEADS)
        qs = jnp.concatenate([q[:, base + c * 128:base + (c + 1) * 128] for c in range(4)], axis=0)
        o = None
        for parity, (kx, vx) in enumerate(((k_lo, v_lo), (k_hi, v_hi))):
            s = lax.dot_general(qs, kx, (((1,), (1,)), ((), ())), preferred_element_type=F32)
            s = s + bias
            sink = sink_ref[kvh * 2 + parity]
            m = jnp.maximum(jnp.max(s, axis=-1, keepdims=True), sink)
            p = jnp.exp(s - m)
            denom = jnp.sum(p, axis=-1, keepdims=True) + jnp.exp(sink - m)
            part = jnp.dot(p.astype(BF16), vx, preferred_element_type=F32) * (1.0 / denom)
            o = part if o is None else o + part
        for c in range(4):
            o_ref[0, :, base + c * 128:base + (c + 1) * 128] = o[c * BLOCK:(c + 1) * BLOCK].astype(BF16)


def _swa_attn(q, kk, vv, sink_rows, batch, lp, s_real):
    nb = s_real // BLOCK
    q3 = q.reshape(batch, lp, A_Q_W)
    k3 = kk.reshape(batch, lp, 256)
    v3 = vv.reshape(batch, lp, 256)
    cur = lambda b, n: (b, n, 0)
    prev = lambda b, n: (b, jnp.maximum(n - 1, 0), 0)
    meta = lambda b, n: (b, s_real // N_META, 0)
    out = pl.pallas_call(
        functools.partial(_swa_attn_kernel, n_real_blocks=nb),
        grid=(batch, nb + 1),
        in_specs=[
            pl.BlockSpec((1, BLOCK, A_Q_W), cur),
            pl.BlockSpec((1, BLOCK, 256), cur),
            pl.BlockSpec((1, BLOCK, 256), prev),
            pl.BlockSpec((1, N_META, 256), meta),
            pl.BlockSpec((1, BLOCK, 256), cur),
            pl.BlockSpec((1, BLOCK, 256), prev),
            pl.BlockSpec((1, N_META, 256), meta),
            pl.BlockSpec((4, 4 * BLOCK, 1), lambda b, n: (0, 0, 0)),
        ],
        out_specs=pl.BlockSpec((1, BLOCK, A_Q_W), cur),
        out_shape=jax.ShapeDtypeStruct((batch, lp, A_Q_W), BF16),
        compiler_params=_params("parallel", "parallel"),
        name="swa_attn",
    )(q3, k3, k3, k3, v3, v3, v3, sink_rows)
    return out.reshape(batch * lp, A_Q_W)


def _route(xn, wr, br):
    lg = jnp.dot(xn, wr, preferred_element_type=F32, precision=lax.Precision.HIGHEST) + br
    lane_i = lax.broadcasted_iota(jnp.int32, lg.shape, 1)
    lane = lane_i.astype(F32)
    lane_grp = (lane_i >> 3).astype(F32)
    big = 1e9
    is_g = lane_grp == float(N_EXPERTS // EXPERTS_PER_GROUP)
    is_g = jnp.logical_and(is_g, lane_i < N_EXPERTS + N_GROUPS)
    gl = jnp.where(is_g, lg, -jnp.inf)
    gmax = jnp.max(gl, axis=-1, keepdims=True)
    g_p = 1.0 / jnp.sum(jnp.exp(gl - gmax), axis=-1, keepdims=True)
    g_idx = jnp.min(jnp.where(gl == gmax, lane - float(N_EXPERTS), big), axis=-1, keepdims=True)
    el = jnp.where(lane_grp == g_idx, lg, -jnp.inf)
    m1 = jnp.max(el, axis=-1, keepdims=True)
    esum = jnp.sum(jnp.exp(el - m1), axis=-1, keepdims=True)
    i1 = jnp.min(jnp.where(el == m1, lane, big), axis=-1, keepdims=True)
    el2 = jnp.where(lane == i1, -jnp.inf, el)
    m2 = jnp.max(el2, axis=-1, keepdims=True)
    i2 = jnp.min(jnp.where(el2 == m2, lane, big), axis=-1, keepdims=True)
    e1 = 1.0 / esum
    e2 = jnp.exp(m2 - m1) / esum
    w1 = g_p * e1 / (e1 + e2)
    w2 = g_p * e2 / (e1 + e2)
    out = jnp.where(lane_i == 0, i1, 0.0)
    out = jnp.where(lane_i == 1, i2, out)
    lane = lane_i
    out = jnp.where(lane == 2, w1, out)
    out = jnp.where(lane == 3, w2, out)
    return out


def _attn_out_kernel(o_ref, w_ref, b_ref, h_ref, g_ref, wr_ref, br_ref, h1_ref, route_ref):
    h1 = h_ref[...] + jnp.dot(o_ref[...], w_ref[...], preferred_element_type=F32) + b_ref[...]
    h1_ref[...] = h1
    route_ref[...] = _route(_rms(h1, g_ref[...]), wr_ref[...], br_ref[...])


def _attn_out(o, w, b, h, g, wr, br):
    tp, ko = o.shape
    row = lambda i: (i, 0)
    fixed = lambda i: (0, 0)
    return pl.pallas_call(
        _attn_out_kernel,
        grid=(tp // ROW_TILE,),
        in_specs=[
            pl.BlockSpec((ROW_TILE, ko), row),
            pl.BlockSpec((ko, D_MODEL), fixed),
            pl.BlockSpec((1, D_MODEL), fixed),
            pl.BlockSpec((ROW_TILE, D_MODEL), row),
            pl.BlockSpec((1, D_MODEL), fixed),
            pl.BlockSpec((D_MODEL, ROUTE_LANES), fixed),
            pl.BlockSpec((1, ROUTE_LANES), fixed),
        ],
        out_specs=[
            pl.BlockSpec((ROW_TILE, D_MODEL), row),
            pl.BlockSpec((ROW_TILE, ROUTE_LANES), row),
        ],
        out_shape=[
            jax.ShapeDtypeStruct((tp, D_MODEL), F32),
            jax.ShapeDtypeStruct((tp, ROUTE_LANES), F32),
        ],
        compiler_params=_params("parallel"),
        name="attn_out_route",
    )(o, w, b, h, g, wr, br)


def _row_copy(src_hbm, src_row, dst, slot, dst_row, sem):
    return pltpu.make_async_copy(src_hbm.at[pl.ds(src_row, 1)], dst.at[slot, pl.ds(dst_row, 1)], sem.at[slot])


def _moe_kernel(be_ref, tok_ref, h_hbm, g_ref, wg_ref, wu_ref, wd_ref, rw_ref, y_ref, buf, sem):
    del be_ref
    i = pl.program_id(0)
    slot = i % 2

    def issue(blk, to_slot):
        def body(r, carry):
            _row_copy(h_hbm, tok_ref[blk * MOE_BLOCK + r], buf, to_slot, r, sem).start()
            return carry
        lax.fori_loop(0, MOE_BLOCK, body, 0)

    @pl.when(i == 0)
    def _():
        issue(0, 0)

    @pl.when(i + 1 < pl.num_programs(0))
    def _():
        issue(i + 1, 1 - slot)

    def wait_row(r, carry):
        _row_copy(h_hbm, 0, buf, slot, r, sem).wait()
        return carry
    lax.fori_loop(0, MOE_BLOCK, wait_row, 0)

    xn = _rms(buf[slot], g_ref[...]).astype(BF16)
    gate = jnp.dot(xn, wg_ref[0], preferred_element_type=F32)
    up = jnp.dot(xn, wu_ref[0], preferred_element_type=F32)
    act = (gate * jax.nn.sigmoid(gate) * up).astype(BF16)
    y_ref[...] = jnp.dot(act, wd_ref[0], preferred_element_type=F32) * rw_ref[...]


def _moe_experts(h1, g, wg, wu, wd, block_expert, src_tok, row_w):
    n_blocks = block_expert.shape[0]
    ew = lambda i, be, tok: (be[i], 0, 0)
    grid_spec = pltpu.PrefetchScalarGridSpec(
        num_scalar_prefetch=2,
        grid=(n_blocks,),
        in_specs=[
            pl.BlockSpec(memory_space=pl.ANY),
            pl.BlockSpec((1, D_MODEL), lambda i, be, tok: (0, 0)),
            pl.BlockSpec((1, D_MODEL, D_EXPERT), ew),
            pl.BlockSpec((1, D_MODEL, D_EXPERT), ew),
            pl.BlockSpec((1, D_EXPERT, D_MODEL), ew),
            pl.BlockSpec((MOE_BLOCK, 1), lambda i, be, tok: (i, 0)),
        ],
        out_specs=pl.BlockSpec((MOE_BLOCK, D_MODEL), lambda i, be, tok: (i, 0)),
        scratch_shapes=[
            pltpu.VMEM((2, MOE_BLOCK, D_MODEL), F32),
            pltpu.SemaphoreType.DMA((2,)),
        ],
    )
    return pl.pallas_call(
        _moe_kernel,
        grid_spec=grid_spec,
        out_shape=jax.ShapeDtypeStruct((n_blocks * MOE_BLOCK, D_MODEL), F32),
        compiler_params=_params("arbitrary"),
        name="moe_experts",
    )(block_expert, src_tok, h1, g, wg, wu, wd, row_w)


def _combine_kernel(pos_ref, h_ref, g_ref, y_hbm, o_ref, buf, sem, *, tiles_per_batch, n_tokens, final):
    b = pl.program_id(0)
    j = pl.program_id(1)
    nj = pl.num_programs(1)
    step = b * nj + j
    slot = step % 2

    def issue(bb, jj, to_slot):
        tile = bb * tiles_per_batch + jj
        def body(r, carry):
            t = tile * BLOCK + r
            _row_copy(y_hbm, pos_ref[t], buf, to_slot, r, sem).start()
            _row_copy(y_hbm, pos_ref[n_tokens + t], buf, to_slot, BLOCK + r, sem).start()
            return carry
        lax.fori_loop(0, BLOCK, body, 0)

    @pl.when(step == 0)
    def _():
        issue(b, j, 0)

    @pl.when(step + 1 < pl.num_programs(0) * nj)
    def _():
        nxt = j + 1
        wrap = nxt == nj
        issue(jnp.where(wrap, b + 1, b), jnp.where(wrap, 0, nxt), 1 - slot)

    def wait_row(r, carry):
        _row_copy(y_hbm, 0, buf, slot, r, sem).wait()
        return carry
    lax.fori_loop(0, 2 * BLOCK, wait_row, 0)

    h2 = h_ref[0] + buf[slot, 0:BLOCK] + buf[slot, BLOCK:2 * BLOCK]
    if final:
        h2 = _rms(h2, g_ref[...])
    o_ref[0] = h2


def _moe_combine(h1, y, pos, g, batch, lp, s_real, final):
    tiles_per_batch = lp // BLOCK
    nj = s_real // BLOCK if final else tiles_per_batch
    out_rows = s_real if final else lp
    tile = lambda b, j, pos: (b, j, 0)
    grid_spec = pltpu.PrefetchScalarGridSpec(
        num_scalar_prefetch=1,
        grid=(batch, nj),
        in_specs=[
            pl.BlockSpec((1, BLOCK, D_MODEL), tile),
            pl.BlockSpec((1, D_MODEL), lambda b, j, pos: (0, 0)),
            pl.BlockSpec(memory_space=pl.ANY),
        ],
        out_specs=pl.BlockSpec((1, BLOCK, D_MODEL), tile),
        scratch_shapes=[
            pltpu.VMEM((2, 2 * BLOCK, D_MODEL), F32),
            pltpu.SemaphoreType.DMA((2,)),
        ],
    )
    return pl.pallas_call(
        functools.partial(_combine_kernel, tiles_per_batch=tiles_per_batch, n_tokens=batch * lp, final=final),
        grid_spec=grid_spec,
        out_shape=jax.ShapeDtypeStruct((batch, out_rows, D_MODEL), F32),
        compiler_params=_params("arbitrary", "arbitrary"),
        name="moe_combine_final" if final else "moe_combine",
    )(pos, h1.reshape(batch, lp, D_MODEL), g, y)


def _dispatch_plan(route):
    tp = route.shape[0]
    ids = route[:, 0:TOP_K].astype(jnp.int32)
    wts = route[:, TOP_K:2 * TOP_K]
    n_slots = tp * TOP_K
    slot_expert = ids.reshape(-1)
    slot_token = jnp.repeat(jnp.arange(tp, dtype=jnp.int32), TOP_K)
    slot_weight = wts.reshape(-1)
    order = jnp.argsort(slot_expert, stable=True)
    se = slot_expert[order]
    counts = jnp.bincount(slot_expert, length=N_EXPERTS)
    starts = jnp.cumsum(counts) - counts
    padded = (counts + MOE_BLOCK - 1) // MOE_BLOCK * MOE_BLOCK
    pends = jnp.cumsum(padded)
    pstarts = pends - padded
    dest = (pstarts[se] + jnp.arange(n_slots) - starts[se]).astype(jnp.int32)
    n_blocks = -(-(n_slots + N_EXPERTS * (MOE_BLOCK - 1)) // MOE_BLOCK)
    n_rows = n_blocks * MOE_BLOCK
    src_tok = jnp.zeros((n_rows,), jnp.int32).at[dest].set(slot_token[order])
    row_w = jnp.zeros((n_rows,), F32).at[dest].set(slot_weight[order])
    block_expert = jnp.minimum(
        jnp.searchsorted(pends, jnp.arange(n_blocks) * MOE_BLOCK, side='right'), N_EXPERTS - 1).astype(jnp.int32)
    slot_dest = jnp.zeros((n_slots,), jnp.int32).at[order].set(dest)
    pos = slot_dest.reshape(tp, TOP_K).T.reshape(-1)
    return block_expert, src_tok, row_w.reshape(n_rows, 1), pos


def _moe_layer(h1, route, g, wg, wu, wd, final_g, batch, lp, s_real, final):
    block_expert, src_tok, row_w, pos = _dispatch_plan(route)
    y = _moe_experts(h1, g, wg, wu, wd, block_expert, src_tok, row_w)
    return _moe_combine(h1, y, pos, final_g, batch, lp, s_real, final)


def _mla_rope(c, cos, sin_a, sin_b):
    half = B_ROPE // 2
    return c * cos + pltpu.roll(c, 128 - half, 1) * sin_a + pltpu.roll(c, half, 1) * sin_b


def _mla_proj_kernel(h_ref, g_ref, win_ref, qn_ref, kvn_ref, wq_ref, wk_ref, wv_ref,
                     qc_ref, qsa_ref, qsb_ref, kc_ref, ksa_ref, ksb_ref, q_ref, k_ref, v_ref):
    xn = _rms(h_ref[...], g_ref[...]).astype(BF16)
    c = jnp.dot(xn, win_ref[...], preferred_element_type=F32)
    cq = _rms(c[:, :B_Q_RANK], qn_ref[...]).astype(BF16)
    ckv = _rms(c[:, B_Q_RANK:B_Q_RANK + B_KV_RANK], kvn_ref[...]).astype(BF16)
    kpe = _mla_rope(c[:, B_Q_RANK + B_KV_RANK:], kc_ref[...], ksa_ref[...], ksb_ref[...])
    q = jnp.dot(cq, wq_ref[...], preferred_element_type=F32)
    k = jnp.dot(ckv, wk_ref[...], preferred_element_type=F32)
    v = jnp.dot(ckv, wv_ref[...], preferred_element_type=F32)
    qc, qsa, qsb = qc_ref[...], qsa_ref[...], qsb_ref[...]
    ones_col = (lax.broadcasted_iota(jnp.int32, kpe.shape, 1) == B_V).astype(F32)
    for hd in range(B_HEADS):
        sl = slice(hd * B_HEAD_PAD, (hd + 1) * B_HEAD_PAD)
        q_ref[0, hd] = _mla_rope(q[:, sl], qc, qsa, qsb).astype(BF16)
        k_ref[0, hd] = (k[:, sl] + kpe).astype(BF16)
        v_ref[0, hd] = (v[:, sl] + ones_col).astype(BF16)


def _mla_proj(h, g, win, qn, kvn, wq, wk, wv, tabs, batch, lp):
    tiles_per_batch = lp // ROW_TILE
    row = lambda b, i: (b * tiles_per_batch + i, 0)
    fixed = lambda b, i: (0, 0)
    tab = lambda b, i: (i, 0)
    head_out = pl.BlockSpec((1, B_HEADS, ROW_TILE, B_HEAD_PAD), lambda b, i: (b, 0, i, 0))
    hw = B_HEADS * B_HEAD_PAD
    shape = jax.ShapeDtypeStruct((batch, B_HEADS, lp, B_HEAD_PAD), BF16)
    return pl.pallas_call(
        _mla_proj_kernel,
        grid=(batch, tiles_per_batch),
        in_specs=[
            pl.BlockSpec((ROW_TILE, D_MODEL), row),
            pl.BlockSpec((1, D_MODEL), fixed),
            pl.BlockSpec((D_MODEL, 512), fixed),
            pl.BlockSpec((1, B_Q_RANK), fixed),
            pl.BlockSpec((1, B_KV_RANK), fixed),
            pl.BlockSpec((B_Q_RANK, hw), fixed),
            pl.BlockSpec((B_KV_RANK, hw), fixed),
            pl.BlockSpec((B_KV_RANK, hw), fixed),
        ] + [pl.BlockSpec((ROW_TILE, 128), tab)] * 6,
        out_specs=[head_out, head_out, head_out],
        out_shape=[shape, shape, shape],
        compiler_params=_params("parallel", "parallel"),
        name="mla_proj",
    )(h, g, win, qn, kvn, wq, wk, wv, *tabs)


def _mla_attn_kernel(q_ref, k_ref, v_ref, o_ref, m_sc, acc_sc, *, s_real, tq, tk):
    qi = pl.program_id(2)
    nq = s_real // tq
    is_meta_q = qi == nq
    q = q_ref[0, 0]
    nt = (((1,), (1,)), ((), ()))

    s = lax.dot_general(q, k_ref[0, 0, s_real:s_real + BLOCK, :], nt, preferred_element_type=F32)
    row = lax.broadcasted_iota(jnp.int32, s.shape, 0)
    col = lax.broadcasted_iota(jnp.int32, s.shape, 1)
    limit = jnp.minimum(jnp.where(is_meta_q, row, N_META - 1), N_META - 1)
    s = jnp.where(col <= limit, s, NEG_INF)
    m0 = jnp.max(s, axis=-1, keepdims=True)
    m_sc[...] = m0
    acc_sc[...] = jnp.dot(jnp.exp(s - m0).astype(BF16), v_ref[0, 0, s_real:s_real + BLOCK, :],
                          preferred_element_type=F32)

    def chunk(j, masked):
        start = pl.multiple_of(j * tk, tk)
        s = lax.dot_general(q, k_ref[0, 0, pl.ds(start, tk), :], nt, preferred_element_type=F32)
        if masked:
            r = lax.broadcasted_iota(jnp.int32, s.shape, 0) + qi * tq
            c = lax.broadcasted_iota(jnp.int32, s.shape, 1) + j * tk
            s = jnp.where(c <= r, s, NEG_INF)
        m_old = m_sc[...]
        m_new = jnp.maximum(m_old, jnp.max(s, axis=-1, keepdims=True))
        p = jnp.exp(s - m_new).astype(BF16)
        acc_sc[...] = acc_sc[...] * jnp.exp(m_old - m_new) + jnp.dot(
            p, v_ref[0, 0, pl.ds(start, tk), :], preferred_element_type=F32)
        m_sc[...] = m_new

    per_q = tq // tk
    n_full = jnp.where(is_meta_q, 0, qi * per_q)
    n_all = jnp.where(is_meta_q, 0, (qi + 1) * per_q)

    def full_body(j, carry):
        chunk(j, False)
        return carry

    def diag_body(j, carry):
        chunk(j, True)
        return carry

    lax.fori_loop(0, n_full, full_body, 0)
    lax.fori_loop(n_full, n_all, diag_body, 0)
    acc = acc_sc[...]
    o_ref[0] = (acc * (1.0 / acc[:, B_V:B_V + 1])).astype(BF16)


def _mla_attn(q, k, v, batch, lp, s_real):
    tq = min(512, s_real)
    tk = tq
    nq = s_real // tq
    qmap = lambda b, h, i: (b, h, i, 0)
    kvmap = lambda b, h, i: (b, h, 0, 0)
    return pl.pallas_call(
        functools.partial(_mla_attn_kernel, s_real=s_real, tq=tq, tk=tk),
        grid=(batch, B_HEADS, nq + 1),
        in_specs=[
            pl.BlockSpec((1, 1, tq, B_HEAD_PAD), qmap),
            pl.BlockSpec((1, 1, lp, B_HEAD_PAD), kvmap),
            pl.BlockSpec((1, 1, lp, B_HEAD_PAD), kvmap),
        ],
        out_specs=pl.BlockSpec((1, tq, B_HEAD_PAD), lambda b, h, i: (b, i, h)),
        out_shape=jax.ShapeDtypeStruct((batch, lp, B_HEADS * B_HEAD_PAD), BF16),
        scratch_shapes=[pltpu.VMEM((tq, 1), F32), pltpu.VMEM((tq, B_HEAD_PAD), F32)],
        compiler_params=_params("parallel", "parallel", "arbitrary"),
        name="mla_attn",
    )(q, k, v)


def _positions(lp, s_real):
    r = jnp.arange(lp, dtype=jnp.int32)
    return jnp.where(r < s_real, r + N_META, r - s_real).astype(F32)


def _rope_angles(pos, dim):
    inv_freq = 1.0 / (ROPE_THETA ** (jnp.arange(0, dim, 2, dtype=F32) / dim))
    ang = pos[:, None] * inv_freq[None, :]
    return jnp.cos(ang), jnp.sin(ang)


def _swa_tables(pos):
    cos, sin = _rope_angles(pos, A_HEAD_DIM)
    cos_t = jnp.tile(cos, (1, 4))
    sin_t = jnp.tile(jnp.concatenate([-sin, sin], axis=1), (1, 2))
    return cos_t, sin_t


def _mla_tables(pos, scale):
    cos, sin = _rope_angles(pos, B_ROPE)
    n = pos.shape[0]
    z16 = jnp.zeros((n, B_ROPE // 2), F32)
    tail = jnp.zeros((n, 128 - B_NOPE - B_ROPE), F32)
    ones = jnp.ones((n, B_NOPE), F32)
    zeros = jnp.zeros((n, B_NOPE), F32)
    cos_t = jnp.concatenate([ones, cos, cos, tail], axis=1) * scale
    sin_a = jnp.concatenate([zeros, -sin, z16, tail], axis=1) * scale
    sin_b = jnp.concatenate([zeros, z16, sin, tail], axis=1) * scale
    return cos_t, sin_a, sin_b


def _pad_heads(w, n_heads, width, pad_to):
    k = w.shape[0]
    w = w.reshape(k, n_heads, width)
    w = jnp.pad(w, ((0, 0), (0, 0), (0, pad_to - width)))
    return w.reshape(k, n_heads * pad_to)


def kernel(x, meta_tokens, attn_norm, ffn_norm, final_norm, a_w_qkv, a_b_qkv, a_sinks, a_w_o, a_b_o,
           b_w_in, b_q_norm, b_kv_norm, b_w_uq, b_w_ukv, b_w_o,
           moe_w_group, moe_b_group, moe_w_router, moe_b_router, moe_w_gate, moe_w_up, moe_w_down):
    batch, s_real, _ = x.shape
    lp = s_real + BLOCK
    tp = batch * lp
    assert tp % ROW_TILE == 0 and lp % ROW_TILE == 0 and s_real % BLOCK == 0

    meta = jnp.broadcast_to(meta_tokens[None].astype(x.dtype), (batch, N_META, D_MODEL))
    pad = jnp.zeros((batch, BLOCK - N_META, D_MODEL), x.dtype)
    h = jnp.concatenate([x, meta, pad], axis=1).reshape(tp, D_MODEL)
    pos = _positions(lp, s_real)

    def router_params(i):
        wr = jnp.concatenate([moe_w_router[i], moe_w_group[i]], axis=1)
        wr = jnp.pad(wr, ((0, 0), (0, ROUTE_LANES - wr.shape[1])))
        br = jnp.concatenate([moe_b_router[i], moe_b_group[i]])
        br = jnp.pad(br, (0, ROUTE_LANES - br.shape[0]))[None]
        return wr, br

    def moe(i, h1, route, final):
        return _moe_layer(h1, route, ffn_norm[i][None], moe_w_gate[i].astype(BF16), moe_w_up[i].astype(BF16),
                          moe_w_down[i].astype(BF16), final_norm[None], batch, lp, s_real, final)

    wq, wk, wv = a_w_qkv[0][:, :A_Q_W], a_w_qkv[0][:, A_Q_W:A_Q_W + A_KV_W], a_w_qkv[0][:, A_Q_W + A_KV_W:]
    bq, bk, bv = a_b_qkv[0][:A_Q_W], a_b_qkv[0][A_Q_W:A_Q_W + A_KV_W], a_b_qkv[0][A_Q_W + A_KV_W:]
    swap = lambda t: jnp.concatenate([t[..., A_HEAD_DIM:], t[..., :A_HEAD_DIM]], axis=-1)
    w_a = jnp.concatenate([wq, wk, swap(wk), wv, swap(wv)], axis=1).astype(BF16)
    b_a = jnp.concatenate([bq, bk, swap(bk), bv, swap(bv)])[None]
    cos_a, sin_a = _swa_tables(pos)
    q, kk, vv = _swa_qkv(h, attn_norm[0][None], w_a, b_a, cos_a, sin_a, lp)
    sink_rows = jnp.repeat(a_sinks[0].astype(F32).reshape(A_KV_HEADS, 4, 2).transpose(0, 2, 1).reshape(4, 4), BLOCK, axis=1)[..., None]
    o = _swa_attn(q, kk, vv, sink_rows, batch, lp, s_real)
    wr, br = router_params(0)
    h1, route = _attn_out(o, a_w_o[0].astype(BF16), a_b_o[0][None], h, ffn_norm[0][None], wr, br)
    h = moe(0, h1, route, False).reshape(tp, D_MODEL)

    scale = (B_NOPE + B_ROPE) ** -0.5
    w_in = b_w_in[0]
    kpe_cols = jnp.pad(w_in[:, B_Q_RANK + B_KV_RANK:], ((0, 0), (B_NOPE, 128 - B_NOPE - B_ROPE)))
    w_in_p = jnp.concatenate([w_in[:, :B_Q_RANK + B_KV_RANK], kpe_cols], axis=1).astype(BF16)
    w_q_p = _pad_heads(b_w_uq[0], B_HEADS, B_NOPE + B_ROPE, B_HEAD_PAD).astype(BF16)
    w_ukv = b_w_ukv[0].reshape(B_KV_RANK, B_HEADS, B_NOPE + B_V)
    w_k_p = _pad_heads(w_ukv[:, :, :B_NOPE].reshape(B_KV_RANK, -1), B_HEADS, B_NOPE, B_HEAD_PAD).astype(BF16)
    w_v_p = _pad_heads(w_ukv[:, :, B_NOPE:].reshape(B_KV_RANK, -1), B_HEADS, B_V, B_HEAD_PAD).astype(BF16)
    tabs = _mla_tables(pos, scale) + _mla_tables(pos, 1.0)
    q, k, v = _mla_proj(h, attn_norm[1][None], w_in_p, b_q_norm[0][None], b_kv_norm[0][None],
                        w_q_p, w_k_p, w_v_p, tabs, batch, lp)
    o = _mla_attn(q, k, v, batch, lp, s_real).reshape(tp, B_HEADS * B_HEAD_PAD)
    w_o_p = jnp.pad(b_w_o[0].reshape(B_HEADS, B_V, D_MODEL), ((0, 0), (0, B_HEAD_PAD - B_V), (0, 0)))
    w_o_p = w_o_p.reshape(B_HEADS * B_HEAD_PAD, D_MODEL).astype(BF16)
    wr, br = router_params(1)
    h1, route = _attn_out(o, w_o_p, jnp.zeros((1, D_MODEL), F32), h, ffn_norm[1][None], wr, br)
    return moe(1, h1, route, True)
```

```python
import functools
import math

import jax
import jax.numpy as jnp
from jax import lax
from jax.experimental import pallas as pl
from jax.experimental.pallas import tpu as pltpu

F32 = jnp.float32
BF16 = jnp.bfloat16

D_MODEL = 1024
N_META = 16
BLOCK = 128
ROPE_THETA = 10000.0
NORM_EPS = 1e-6
NEG_INF = -1e30

A_HEADS = 16
A_KV_HEADS = 2
A_HEAD_DIM = 64
A_Q_W = A_HEADS * A_HEAD_DIM
A_KV_W = A_KV_HEADS * A_HEAD_DIM

B_HEADS = 16
B_NOPE = 64
B_ROPE = 32
B_V = 64
B_Q_RANK = 256
B_KV_RANK = 128
B_HEAD_PAD = 128

N_GROUPS = 4
EXPERTS_PER_GROUP = 8
N_EXPERTS = N_GROUPS * EXPERTS_PER_GROUP
TOP_K = 2
D_EXPERT = 256
MOE_BLOCK = 128
ROUTE_LANES = 128

ROW_TILE = 640
MLA_Q_TILE = 1024
MLA_SUB_ROWS = 256
VMEM_LIMIT = 56 * 1024 * 1024


def _rms(x, g):
    return x * lax.rsqrt(jnp.mean(x * x, axis=-1, keepdims=True) + NORM_EPS) * g


def _params(*sem):
    return pltpu.CompilerParams(dimension_semantics=sem, vmem_limit_bytes=VMEM_LIMIT)


def _swa_qkv_kernel(h_ref, g_ref, w_ref, b_ref, cos_ref, sin_ref, q_ref, k_ref, v_ref):
    xn = _rms(h_ref[...], g_ref[...]).astype(BF16)
    y = jnp.dot(xn, w_ref[...], preferred_element_type=F32) + b_ref[...]
    cos = cos_ref[...]
    sin = sin_ref[...]
    lane = lax.broadcasted_iota(jnp.int32, cos.shape, 1)
    first = (lane & (A_HEAD_DIM - 1)) < (A_HEAD_DIM // 2)

    def rope(c):
        rot = jnp.where(first, pltpu.roll(c, 128 - A_HEAD_DIM // 2, 1), pltpu.roll(c, A_HEAD_DIM // 2, 1))
        return c * cos + rot * sin

    for j in range(A_Q_W // 128):
        q_ref[:, j * 128:(j + 1) * 128] = (rope(y[:, j * 128:(j + 1) * 128]) * (A_HEAD_DIM ** -0.5)).astype(BF16)
    k_ref[:, 0:128] = rope(y[:, A_Q_W:A_Q_W + 128]).astype(BF16)
    k_ref[:, 128:256] = rope(y[:, A_Q_W + 128:A_Q_W + 256]).astype(BF16)
    v_ref[...] = y[:, A_Q_W + 256:A_Q_W + 512].astype(BF16)


def _swa_qkv(h, g, w, b, cos, sin, lp):
    tp = h.shape[0]
    tiles_per_batch = lp // ROW_TILE
    n_out = w.shape[1]
    row = lambda i: (i, 0)
    fixed = lambda i: (0, 0)
    tab = lambda i: (i % tiles_per_batch, 0)
    return pl.pallas_call(
        _swa_qkv_kernel,
        grid=(tp // ROW_TILE,),
        in_specs=[
            pl.BlockSpec((ROW_TILE, D_MODEL), row),
            pl.BlockSpec((1, D_MODEL), fixed),
            pl.BlockSpec((D_MODEL, n_out), fixed),
            pl.BlockSpec((1, n_out), fixed),
            pl.BlockSpec((ROW_TILE, 128), tab),
            pl.BlockSpec((ROW_TILE, 128), tab),
        ],
        out_specs=[
            pl.BlockSpec((ROW_TILE, A_Q_W), row),
            pl.BlockSpec((ROW_TILE, 256), row),
            pl.BlockSpec((ROW_TILE, 256), row),
        ],
        out_shape=[
            jax.ShapeDtypeStruct((tp, A_Q_W), BF16),
            jax.ShapeDtypeStruct((tp, 256), BF16),
            jax.ShapeDtypeStruct((tp, 256), BF16),
        ],
        compiler_params=_params("parallel"),
        name="swa_qkv",
    )(h, g, w, b, cos, sin)


def _swa_attn_kernel(q_ref, kc_ref, kp_ref, km_ref, vc_ref, vp_ref, vm_ref, sink_ref, o_ref, *, n_real_blocks):
    n = pl.program_id(1)
    is_real = n < n_real_blocks
    n_keys = N_META + 2 * BLOCK
    big = jnp.int32(1 << 20)
    row = lax.broadcasted_iota(jnp.int32, (BLOCK, n_keys), 0)
    col = lax.broadcasted_iota(jnp.int32, (BLOCK, n_keys), 1)
    jp = col - N_META
    jc = col - (N_META + BLOCK)
    meta_lim = jnp.where(is_real, N_META, 0)
    prev_off = jnp.where(jnp.logical_and(is_real, n >= 1), 0, big)
    cur_lim = jnp.where(is_real, big, N_META)
    ok = (col < meta_lim) | ((jc < 0) & (jp > row + prev_off)) | ((jc >= 0) & (jc <= row) & (jc < cur_lim))
    bias1 = jnp.where(ok, 0.0, NEG_INF).astype(F32)
    bias = jnp.concatenate([bias1] * 4, axis=0)

    kcat = jnp.concatenate([km_ref[0], kp_ref[0], kc_ref[0]], axis=0)
    vcat = jnp.concatenate([vm_ref[0], vp_ref[0], vc_ref[0]], axis=0)
    lo = lax.broadcasted_iota(jnp.int32, (n_keys, 128), 1) < A_HEAD_DIM
    zero = jnp.zeros((n_keys, 128), BF16)
    q = q_ref[0]

    for kvh in range(A_KV_HEADS):
        a, b = (kcat[:, :128], kcat[:, 128:]) if kvh == 0 else (kcat[:, 128:], kcat[:, :128])
        k_lo = jnp.where(lo, a, zero)
        k_hi = jnp.where(lo, zero, b)
        a, b = (vcat[:, :128], vcat[:, 128:]) if kvh == 0 else (vcat[:, 128:], vcat[:, :128])
        v_lo = jnp.where(lo, a, zero)
        v_hi = jnp.where(lo, zero, b)
        base = kvh * (A_Q_W // A_KV_HEADS)
        qs = jnp.concatenate([q[:, base + c * 128:base + (c + 1) * 128] for c in range(4)], axis=0)
        o = None
        for parity, (kx, vx) in enumerate(((k_lo, v_lo), (k_hi, v_hi))):
            s = lax.dot_general(qs, kx, (((1,), (1,)), ((), ())), preferred_element_type=F32)
            s = s + bias
            sink = sink_ref[kvh * 2 + parity]
            m = jnp.maximum(jnp.max(s, axis=-1, keepdims=True), sink)
            p = jnp.exp(s - m)
            denom = jnp.sum(p, axis=-1, keepdims=True) + jnp.exp(sink - m)
            part = jnp.dot(p.astype(BF16), vx, preferred_element_type=F32) * (1.0 / denom)
            o = part if o is None else o + part
        for c in range(4):
            o_ref[0, :, base + c * 128:base + (c + 1) * 128] = o[c * BLOCK:(c + 1) * BLOCK].astype(BF16)


def _swa_attn(q, kk, vv, sink_rows, batch, lp, s_real):
    nb = s_real // BLOCK
    q3 = q.reshape(batch, lp, A_Q_W)
    k3 = kk.reshape(batch, lp, 256)
    v3 = vv.reshape(batch, lp, 256)
    cur = lambda b, n: (b, n, 0)
    prev = lambda b, n: (b, jnp.maximum(n - 1, 0), 0)
    meta = lambda b, n: (b, s_real // N_META, 0)
    out = pl.pallas_call(
        functools.partial(_swa_attn_kernel, n_real_blocks=nb),
        grid=(batch, nb + 1),
        in_specs=[
            pl.BlockSpec((1, BLOCK, A_Q_W), cur),
            pl.BlockSpec((1, BLOCK, 256), cur),
            pl.BlockSpec((1, BLOCK, 256), prev),
            pl.BlockSpec((1, N_META, 256), meta),
            pl.BlockSpec((1, BLOCK, 256), cur),
            pl.BlockSpec((1, BLOCK, 256), prev),
            pl.BlockSpec((1, N_META, 256), meta),
            pl.BlockSpec((4, 4 * BLOCK, 1), lambda b, n: (0, 0, 0)),
        ],
        out_specs=pl.BlockSpec((1, BLOCK, A_Q_W), cur),
        out_shape=jax.ShapeDtypeStruct((batch, lp, A_Q_W), BF16),
        compiler_params=_params("parallel", "parallel"),
        name="swa_attn",
    )(q3, k3, k3, k3, v3, v3, v3, sink_rows)
    return out.reshape(batch * lp, A_Q_W)


def _route(xn, wr, br):
    lg = jnp.dot(xn, wr, preferred_element_type=F32, precision=lax.Precision.HIGHEST) + br
    lane_i = lax.broadcasted_iota(jnp.int32, lg.shape, 1)
    lane = lane_i.astype(F32)
    lane_grp = (lane_i >> 3).astype(F32)
    big = 1e9
    is_g = lane_grp == float(N_EXPERTS // EXPERTS_PER_GROUP)
    is_g = jnp.logical_and(is_g, lane_i < N_EXPERTS + N_GROUPS)
    gl = jnp.where(is_g, lg, -jnp.inf)
    gmax = jnp.max(gl, axis=-1, keepdims=True)
    g_p = 1.0 / jnp.sum(jnp.exp(gl - gmax), axis=-1, keepdims=True)
    g_idx = jnp.min(jnp.where(gl == gmax, lane - float(N_EXPERTS), big), axis=-1, keepdims=True)
    el = jnp.where(lane_grp == g_idx, lg, -jnp.inf)
    m1 = jnp.max(el, axis=-1, keepdims=True)
    esum = jnp.sum(jnp.exp(el - m1), axis=-1, keepdims=True)
    i1 = jnp.min(jnp.where(el == m1, lane, big), axis=-1, keepdims=True)
    el2 = jnp.where(lane == i1, -jnp.inf, el)
    m2 = jnp.max(el2, axis=-1, keepdims=True)
    i2 = jnp.min(jnp.where(el2 == m2, lane, big), axis=-1, keepdims=True)
    e1 = 1.0 / esum
    e2 = jnp.exp(m2 - m1) / esum
    w1 = g_p * e1 / (e1 + e2)
    w2 = g_p * e2 / (e1 + e2)
    out = jnp.where(lane_i == 0, i1, 0.0)
    out = jnp.where(lane_i == 1, i2, out)
    lane = lane_i
    out = jnp.where(lane == 2, w1, out)
    out = jnp.where(lane == 3, w2, out)
    return out


def _attn_out_kernel(o_ref, w_ref, b_ref, h_ref, g_ref, wr_ref, br_ref, h1_ref, route_ref, route_t_ref):
    h1 = h_ref[...] + jnp.dot(o_ref[...], w_ref[...], preferred_element_type=F32) + b_ref[...]
    h1_ref[...] = h1
    route = _route(_rms(h1, g_ref[...]), wr_ref[...], br_ref[...])
    route_ref[...] = route
    route_t_ref[...] = route.T[0:8]


def _attn_out(o, w, b, h, g, wr, br):
    tp, ko = o.shape
    row = lambda i: (i, 0)
    fixed = lambda i: (0, 0)
    return pl.pallas_call(
        _attn_out_kernel,
        grid=(tp // ROW_TILE,),
        in_specs=[
            pl.BlockSpec((ROW_TILE, ko), row),
            pl.BlockSpec((ko, D_MODEL), fixed),
            pl.BlockSpec((1, D_MODEL), fixed),
            pl.BlockSpec((ROW_TILE, D_MODEL), row),
            pl.BlockSpec((1, D_MODEL), fixed),
            pl.BlockSpec((D_MODEL, ROUTE_LANES), fixed),
            pl.BlockSpec((1, ROUTE_LANES), fixed),
        ],
        out_specs=[
            pl.BlockSpec((ROW_TILE, D_MODEL), row),
            pl.BlockSpec((ROW_TILE, ROUTE_LANES), row),
            pl.BlockSpec((8, ROW_TILE), lambda i: (0, i)),
        ],
        out_shape=[
            jax.ShapeDtypeStruct((tp, D_MODEL), F32),
            jax.ShapeDtypeStruct((tp, ROUTE_LANES), F32),
            jax.ShapeDtypeStruct((8, tp), F32),
        ],
        compiler_params=_params("parallel"),
        name="attn_out_route",
    )(o, w, b, h, g, wr, br)


def _plan_kernel(route_ref, rt_ref, pos_ref, be_ref, pends_ref, cnt_row, cnt_col, carry, *, n_blocks_pad):
    phase = pl.program_id(0)
    i = pl.program_id(1)
    tm = rt_ref.shape[1]
    id0 = rt_ref[0:1, :]
    id1 = rt_ref[1:2, :]
    e_sub = lax.broadcasted_iota(jnp.int32, (N_EXPERTS, tm), 0).astype(F32)
    hit0 = e_sub == id0
    hit1 = e_sub == id1
    member_t = jnp.where(hit0, 1.0, jnp.where(hit1, 1.0, 0.0))
    tile_cnt_col = jnp.sum(member_t, axis=1, keepdims=True)

    @pl.when(jnp.logical_and(phase == 0, i == 0))
    def _():
        cnt_row[...] = jnp.zeros(cnt_row.shape, F32)
        cnt_col[...] = jnp.zeros(cnt_col.shape, F32)

    @pl.when(phase == 0)
    def _():
        r = route_ref[...]
        lane = lax.broadcasted_iota(jnp.int32, r.shape, 1).astype(F32)
        member = jnp.where(lane == r[:, 0:1], 1.0, jnp.where(lane == r[:, 1:2], 1.0, 0.0))
        cnt_row[...] += jnp.sum(member, axis=0, keepdims=True)
        cnt_col[...] += jnp.broadcast_to(tile_cnt_col, cnt_col.shape)

    @pl.when(jnp.logical_and(phase == 1, i == 0))
    def _():
        pad_to_block = lambda c: jnp.floor((c + (MOE_BLOCK - 1.0)) * (1.0 / MOE_BLOCK)) * MOE_BLOCK
        padded_row = pad_to_block(cnt_row[...])
        padded_col = pad_to_block(cnt_col[...])
        lane_e = lax.broadcasted_iota(jnp.int32, (N_EXPERTS, 128), 1)
        sub_e = lax.broadcasted_iota(jnp.int32, (N_EXPERTS, 128), 0)
        pstart_col = jnp.sum(jnp.where(lane_e < sub_e, padded_row, 0.0), axis=1, keepdims=True)
        pends_row = jnp.sum(jnp.where(sub_e <= lane_e, padded_col, 0.0), axis=0, keepdims=True)
        pends_col = pstart_col + padded_col[:, 0:1]
        carry[...] = jnp.broadcast_to(pstart_col, carry.shape)
        pends_ref[...] = pends_row.astype(jnp.int32)
        blk_lane = lax.broadcasted_iota(jnp.int32, (N_EXPERTS, n_blocks_pad), 1)
        blk_start = (blk_lane * MOE_BLOCK).astype(F32)
        be = jnp.sum(jnp.where(pends_col <= blk_start, 1.0, 0.0), axis=0, keepdims=True)
        be = jnp.minimum(be, N_EXPERTS - 1.0)
        n_used = pends_row[:, N_EXPERTS - 1:N_EXPERTS] * (1.0 / MOE_BLOCK)
        be = jnp.where(blk_lane[0:1] == n_blocks_pad - 1, n_used, be)
        be_ref[...] = be.astype(jnp.int32)

    @pl.when(phase == 1)
    def _():
        before = lax.broadcasted_iota(jnp.int32, (tm, tm), 0) < lax.broadcasted_iota(jnp.int32, (tm, tm), 1)
        prefix = jnp.dot(member_t.astype(BF16), jnp.where(before, 1.0, 0.0).astype(BF16),
                         preferred_element_type=F32)
        row_of = prefix + carry[:, 0:1]
        dest0 = jnp.sum(jnp.where(hit0, row_of, 0.0), axis=0, keepdims=True)
        dest1 = jnp.sum(jnp.where(hit1, row_of, 0.0), axis=0, keepdims=True)
        pos_ref[...] = jnp.concatenate([dest0, dest1], axis=0).astype(jnp.int32)
        carry[...] += jnp.broadcast_to(tile_cnt_col, carry.shape)


def _dispatch_plan(route, route_t, n_blocks):
    tp = route.shape[0]
    n_tiles = tp // ROW_TILE
    n_blocks_pad = -(-(n_blocks + 1) // 128) * 128
    pos, be, pends = pl.pallas_call(
        functools.partial(_plan_kernel, n_blocks_pad=n_blocks_pad),
        grid=(2, n_tiles),
        in_specs=[
            pl.BlockSpec((ROW_TILE, ROUTE_LANES), lambda p, i: (i, 0)),
            pl.BlockSpec((8, ROW_TILE), lambda p, i: (0, i)),
        ],
        out_specs=[
            pl.BlockSpec((TOP_K, ROW_TILE), lambda p, i: (0, i * p)),
            pl.BlockSpec((1, n_blocks_pad), lambda p, i: (0, 0)),
            pl.BlockSpec((1, 128), lambda p, i: (0, 0)),
        ],
        out_shape=[
            jax.ShapeDtypeStruct((TOP_K, tp), jnp.int32),
            jax.ShapeDtypeStruct((1, n_blocks_pad), jnp.int32),
            jax.ShapeDtypeStruct((1, 128), jnp.int32),
        ],
        scratch_shapes=[pltpu.VMEM((1, 128), F32), pltpu.VMEM((N_EXPERTS, 128), F32), pltpu.VMEM((N_EXPERTS, 128), F32)],
        compiler_params=_params("arbitrary", "arbitrary"),
        name="moe_plan",
    )(route, route_t)
    return pos.reshape(-1), be.reshape(-1), pends.reshape(-1)


def _scatter_row(stage, slot, r, buf_hbm, dst_row, sem):
    return pltpu.make_async_copy(stage.at[slot, pl.ds(r, 1)], buf_hbm.at[pl.ds(dst_row, 1)], sem.at[slot])


def _dispatch_kernel(pos_ref, pends_ref, h_ref, buf_hbm, stage, zsem, sem, *, n_tokens, n_blocks):
    i = pl.program_id(0)
    slot = i % 2

    def zero_block(row0):
        return pltpu.make_async_copy(stage.at[0], buf_hbm.at[pl.ds(pl.multiple_of(row0, MOE_BLOCK), MOE_BLOCK)], zsem)

    def zero_tail(e):
        return zero_block(jnp.maximum(pends_ref[e] - MOE_BLOCK, 0))

    @pl.when(i == 0)
    def _():
        stage[0] = jnp.zeros((BLOCK, D_MODEL), F32)
        for e in range(N_EXPERTS):
            zero_tail(e).start()
        for e in range(N_EXPERTS):
            zero_tail(e).wait()
        first_unused = pends_ref[N_EXPERTS - 1] // MOE_BLOCK

        def start_unused(b, carry):
            zero_block(b * MOE_BLOCK).start()
            return carry

        def wait_unused(b, carry):
            zero_block(b * MOE_BLOCK).wait()
            return carry
        lax.fori_loop(first_unused, n_blocks, start_unused, 0)
        lax.fori_loop(first_unused, n_blocks, wait_unused, 0)

    stage[slot] = h_ref[...]

    def issue(r, carry):
        t = i * BLOCK + r
        _scatter_row(stage, slot, r, buf_hbm, pos_ref[t], sem).start()
        _scatter_row(stage, slot, r, buf_hbm, pos_ref[n_tokens + t], sem).start()
        return carry
    lax.fori_loop(0, BLOCK, issue, 0)

    def drain(which):
        def body(r, carry):
            _scatter_row(stage, which, 0, buf_hbm, 0, sem).wait()
            return carry
        lax.fori_loop(0, 2 * BLOCK, body, 0)

    @pl.when(i > 0)
    def _():
        drain(1 - slot)

    @pl.when(i == pl.num_programs(0) - 1)
    def _():
        drain(slot)


def _moe_dispatch(h1, pos, pends, n_rows):
    tp = h1.shape[0]
    grid_spec = pltpu.PrefetchScalarGridSpec(
        num_scalar_prefetch=2,
        grid=(tp // BLOCK,),
        in_specs=[pl.BlockSpec((BLOCK, D_MODEL), lambda i, pos, pends: (i, 0))],
        out_specs=pl.BlockSpec(memory_space=pl.ANY),
        scratch_shapes=[
            pltpu.VMEM((2, BLOCK, D_MODEL), F32),
            pltpu.SemaphoreType.DMA(()),
            pltpu.SemaphoreType.DMA((2,)),
        ],
    )
    return pl.pallas_call(
        functools.partial(_dispatch_kernel, n_tokens=tp, n_blocks=n_rows // MOE_BLOCK),
        grid_spec=grid_spec,
        out_shape=jax.ShapeDtypeStruct((n_rows, D_MODEL), F32),
        compiler_params=_params("arbitrary"),
        name="moe_dispatch",
    )(pos, pends, h1)


def _moe_kernel(be_ref, x_ref, g_ref, wg_ref, wu_ref, wd_ref, y_ref, *, n_blocks_pad):
    used = pl.program_id(0) < be_ref[n_blocks_pad - 1]

    @pl.when(used)
    def _():
        xn = _rms(x_ref[...], g_ref[...]).astype(BF16)
        gate = jnp.dot(xn, wg_ref[0], preferred_element_type=F32)
        up = jnp.dot(xn, wu_ref[0], preferred_element_type=F32)
        act = (gate * jax.nn.sigmoid(gate) * up).astype(BF16)
        y_ref[...] = jnp.dot(act, wd_ref[0], preferred_element_type=F32)

    @pl.when(jnp.logical_not(used))
    def _():
        y_ref[...] = jnp.zeros(y_ref.shape, F32)


def _moe_experts(buf, g, wg, wu, wd, block_expert, n_blocks):
    n_blocks_pad = block_expert.shape[0]
    blk = lambda i, be: jnp.minimum(i, be[n_blocks_pad - 1] - 1)
    rows = lambda i, be: (blk(i, be), 0)
    ew = lambda i, be: (be[blk(i, be)], 0, 0)
    out_rows = lambda i, be: (i, 0)
    grid_spec = pltpu.PrefetchScalarGridSpec(
        num_scalar_prefetch=1,
        grid=(n_blocks,),
        in_specs=[
            pl.BlockSpec((MOE_BLOCK, D_MODEL), rows),
            pl.BlockSpec((1, D_MODEL), lambda i, be: (0, 0)),
            pl.BlockSpec((1, D_MODEL, D_EXPERT), ew),
            pl.BlockSpec((1, D_MODEL, D_EXPERT), ew),
            pl.BlockSpec((1, D_EXPERT, D_MODEL), ew),
        ],
        out_specs=pl.BlockSpec((MOE_BLOCK, D_MODEL), out_rows),
    )
    return pl.pallas_call(
        functools.partial(_moe_kernel, n_blocks_pad=n_blocks_pad),
        grid_spec=grid_spec,
        out_shape=jax.ShapeDtypeStruct((n_blocks * MOE_BLOCK, D_MODEL), F32),
        compiler_params=_params("arbitrary"),
        name="moe_experts",
    )(block_expert, buf, g, wg, wu, wd)


def _row_copy(src_hbm, src_row, dst, slot, dst_row, sem):
    return pltpu.make_async_copy(src_hbm.at[pl.ds(src_row, 1)], dst.at[slot, pl.ds(dst_row, 1)], sem.at[slot])


def _combine_kernel(pos_ref, h_ref, route_ref, g_ref, y_hbm, o_ref, buf, sem, *, tiles_per_batch, n_tokens, final):
    b = pl.program_id(0)
    j = pl.program_id(1)
    nj = pl.num_programs(1)
    step = b * nj + j
    slot = step % 2

    def issue(bb, jj, to_slot):
        tile = bb * tiles_per_batch + jj
        def body(r, carry):
            t = tile * BLOCK + r
            _row_copy(y_hbm, pos_ref[t], buf, to_slot, r, sem).start()
            _row_copy(y_hbm, pos_ref[n_tokens + t], buf, to_slot, BLOCK + r, sem).start()
            return carry
        lax.fori_loop(0, BLOCK, body, 0)

    @pl.when(step == 0)
    def _():
        issue(b, j, 0)

    @pl.when(step + 1 < pl.num_programs(0) * nj)
    def _():
        nxt = j + 1
        wrap = nxt == nj
        issue(jnp.where(wrap, b + 1, b), jnp.where(wrap, 0, nxt), 1 - slot)

    def wait_row(r, carry):
        _row_copy(y_hbm, 0, buf, slot, r, sem).wait()
        return carry
    lax.fori_loop(0, 2 * BLOCK, wait_row, 0)

    w = route_ref[0]
    h2 = h_ref[0] + w[:, TOP_K:TOP_K + 1] * buf[slot, 0:BLOCK] + w[:, TOP_K + 1:TOP_K + 2] * buf[slot, BLOCK:2 * BLOCK]
    if final:
        h2 = _rms(h2, g_ref[...])
    o_ref[0] = h2


def _moe_combine(h1, route, y, pos, g, batch, lp, s_real, final):
    tiles_per_batch = lp // BLOCK
    nj = s_real // BLOCK if final else tiles_per_batch
    out_rows = s_real if final else lp
    tile = lambda b, j, pos: (b, j, 0)
    grid_spec = pltpu.PrefetchScalarGridSpec(
        num_scalar_prefetch=1,
        grid=(batch, nj),
        in_specs=[
            pl.BlockSpec((1, BLOCK, D_MODEL), tile),
            pl.BlockSpec((1, BLOCK, ROUTE_LANES), tile),
            pl.BlockSpec((1, D_MODEL), lambda b, j, pos: (0, 0)),
            pl.BlockSpec(memory_space=pl.ANY),
        ],
        out_specs=pl.BlockSpec((1, BLOCK, D_MODEL), tile),
        scratch_shapes=[
            pltpu.VMEM((2, 2 * BLOCK, D_MODEL), F32),
            pltpu.SemaphoreType.DMA((2,)),
        ],
    )
    return pl.pallas_call(
        functools.partial(_combine_kernel, tiles_per_batch=tiles_per_batch, n_tokens=batch * lp, final=final),
        grid_spec=grid_spec,
        out_shape=jax.ShapeDtypeStruct((batch, out_rows, D_MODEL), F32),
        compiler_params=_params("arbitrary", "arbitrary"),
        name="moe_combine_final" if final else "moe_combine",
    )(pos, h1.reshape(batch, lp, D_MODEL), route.reshape(batch, lp, ROUTE_LANES), g, y)


def _moe_layer(h1, route, route_t, g, wg, wu, wd, final_g, batch, lp, s_real, final):
    tp = batch * lp
    n_blocks = -(-(tp * TOP_K + N_EXPERTS * (MOE_BLOCK - 1)) // MOE_BLOCK)
    pos, block_expert, pends = _dispatch_plan(route, route_t, n_blocks)
    buf = _moe_dispatch(h1, pos, pends, n_blocks * MOE_BLOCK)
    y = _moe_experts(buf, g, wg, wu, wd, block_expert, n_blocks)
    return _moe_combine(h1, route, y, pos, final_g, batch, lp, s_real, final)


def _mla_rope(c, cos, sin_a, sin_b):
    half = B_ROPE // 2
    return c * cos + pltpu.roll(c, 128 - half, 1) * sin_a + pltpu.roll(c, half, 1) * sin_b


def _mla_proj_kernel(h_ref, g_ref, win_ref, qn_ref, kvn_ref, wq_ref, wk_ref, wv_ref,
                     qc_ref, qsa_ref, qsb_ref, kc_ref, ksa_ref, ksb_ref, q_ref, k_ref, v_ref):
    xn = _rms(h_ref[...], g_ref[...]).astype(BF16)
    c = jnp.dot(xn, win_ref[...], preferred_element_type=F32)
    cq = _rms(c[:, :B_Q_RANK], qn_ref[...]).astype(BF16)
    ckv = _rms(c[:, B_Q_RANK:B_Q_RANK + B_KV_RANK], kvn_ref[...]).astype(BF16)
    kpe = _mla_rope(c[:, B_Q_RANK + B_KV_RANK:], kc_ref[...], ksa_ref[...], ksb_ref[...])
    q = jnp.dot(cq, wq_ref[...], preferred_element_type=F32)
    k = jnp.dot(ckv, wk_ref[...], preferred_element_type=F32)
    v = jnp.dot(ckv, wv_ref[...], preferred_element_type=F32)
    qc, qsa, qsb = qc_ref[...], qsa_ref[...], qsb_ref[...]
    ones_col = (lax.broadcasted_iota(jnp.int32, kpe.shape, 1) == B_V).astype(F32)
    for hd in range(B_HEADS):
        sl = slice(hd * B_HEAD_PAD, (hd + 1) * B_HEAD_PAD)
        q_ref[0, hd] = _mla_rope(q[:, sl], qc, qsa, qsb).astype(BF16)
        k_ref[0, hd] = (k[:, sl] + kpe).astype(BF16)
        v_ref[0, hd] = (v[:, sl] + ones_col).astype(BF16)


def _mla_proj(h, g, win, qn, kvn, wq, wk, wv, tabs, batch, lp):
    tiles_per_batch = lp // ROW_TILE
    row = lambda b, i: (b * tiles_per_batch + i, 0)
    fixed = lambda b, i: (0, 0)
    tab = lambda b, i: (i, 0)
    head_out = pl.BlockSpec((1, B_HEADS, ROW_TILE, B_HEAD_PAD), lambda b, i: (b, 0, i, 0))
    hw = B_HEADS * B_HEAD_PAD
    shape = jax.ShapeDtypeStruct((batch, B_HEADS, lp, B_HEAD_PAD), BF16)
    return pl.pallas_call(
        _mla_proj_kernel,
        grid=(batch, tiles_per_batch),
        in_specs=[
            pl.BlockSpec((ROW_TILE, D_MODEL), row),
            pl.BlockSpec((1, D_MODEL), fixed),
            pl.BlockSpec((D_MODEL, 512), fixed),
            pl.BlockSpec((1, B_Q_RANK), fixed),
            pl.BlockSpec((1, B_KV_RANK), fixed),
            pl.BlockSpec((B_Q_RANK, hw), fixed),
            pl.BlockSpec((B_KV_RANK, hw), fixed),
            pl.BlockSpec((B_KV_RANK, hw), fixed),
        ] + [pl.BlockSpec((ROW_TILE, 128), tab)] * 6,
        out_specs=[head_out, head_out, head_out],
        out_shape=[shape, shape, shape],
        compiler_params=_params("parallel", "parallel"),
        name="mla_proj",
    )(h, g, win, qn, kvn, wq, wk, wv, *tabs)


def _mla_attn_kernel(q_ref, k_ref, v_ref, o_ref, m_sc, acc_sc, *, s_real, tq, tk):
    qi = pl.program_id(2)
    nq = s_real // tq
    is_meta_q = qi == nq
    nt = (((1,), (1,)), ((), ()))
    sub = min(MLA_SUB_ROWS, tq)
    n_sub = tq // sub

    m_sc[...] = jnp.full(m_sc.shape, NEG_INF, F32)
    acc_sc[...] = jnp.zeros(acc_sc.shape, F32)

    def item(r, key0, width, mask_fn, with_meta=False):
        rows = slice(r * sub, (r + 1) * sub)
        k = k_ref[0, 0, pl.ds(key0, width), :]
        if with_meta:
            k = jnp.concatenate([k_ref[0, 0, s_real:s_real + BLOCK, :], k], axis=0)
        s = lax.dot_general(q_ref[0, 0, rows, :], k, nt, preferred_element_type=F32)
        if mask_fn is not None:
            rr = lax.broadcasted_iota(jnp.int32, s.shape, 0) + r * sub
            cc = lax.broadcasted_iota(jnp.int32, s.shape, 1)
            s = jnp.where(mask_fn(rr, cc), s, NEG_INF)

        def finish():
            v = v_ref[0, 0, pl.ds(key0, width), :]
            if with_meta:
                v = jnp.concatenate([v_ref[0, 0, s_real:s_real + BLOCK, :], v], axis=0)
            m_old = m_sc[rows]
            m_new = jnp.maximum(m_old, jnp.max(s, axis=-1, keepdims=True))
            p = jnp.exp2(s - jnp.concatenate([m_new] * (s.shape[1] // 128), axis=1)).astype(BF16)
            acc_sc[rows] = acc_sc[rows] * jnp.exp2(m_old - m_new) + jnp.dot(p, v, preferred_element_type=F32)
            m_sc[rows] = m_new
        return finish

    def run(items):
        pending = None
        for make in items:
            nxt = make()
            if pending is not None:
                pending()
            pending = nxt
        pending()

    def full_body(j, carry):
        start = pl.multiple_of(j * tk, tk)
        run([functools.partial(item, r, start, tk, None) for r in range(n_sub)])
        return carry

    n_full = jnp.where(is_meta_q, 0, qi * (tq // tk))
    lax.fori_loop(0, n_full, full_body, 0)

    @pl.when(jnp.logical_not(is_meta_q))
    def _():
        base = pl.multiple_of(qi * tq, tq)
        diag_mask = lambda rr, cc: jnp.where(cc < BLOCK, cc, cc - BLOCK) <= jnp.where(cc < BLOCK, N_META - 1, rr)
        run([functools.partial(item, r, base, (r + 1) * sub, diag_mask, True) for r in range(n_sub)])
        acc = acc_sc[...]
        o_ref[0] = (acc * (1.0 / acc[:, B_V:B_V + 1])).astype(BF16)

    @pl.when(is_meta_q)
    def _():
        rows = min(sub, BLOCK)
        item(0, s_real, BLOCK, lambda rr, cc: (cc <= rr) & (cc < N_META))()
        acc = acc_sc[0:rows]
        o_ref[0, 0:rows] = (acc * (1.0 / acc[:, B_V:B_V + 1])).astype(BF16)


def _mla_attn(q, k, v, batch, lp, s_real):
    tq = min(MLA_Q_TILE, s_real)
    tk = tq
    nq = s_real // tq
    qmap = lambda b, h, i: (b, h, i, 0)
    kvmap = lambda b, h, i: (b, h, 0, 0)
    return pl.pallas_call(
        functools.partial(_mla_attn_kernel, s_real=s_real, tq=tq, tk=tk),
        grid=(batch, B_HEADS, nq + 1),
        in_specs=[
            pl.BlockSpec((1, 1, tq, B_HEAD_PAD), qmap),
            pl.BlockSpec((1, 1, lp, B_HEAD_PAD), kvmap),
            pl.BlockSpec((1, 1, lp, B_HEAD_PAD), kvmap),
        ],
        out_specs=pl.BlockSpec((1, tq, B_HEAD_PAD), lambda b, h, i: (b, i, h)),
        out_shape=jax.ShapeDtypeStruct((batch, lp, B_HEADS * B_HEAD_PAD), BF16),
        scratch_shapes=[pltpu.VMEM((tq, 128), F32), pltpu.VMEM((tq, B_HEAD_PAD), F32)],
        compiler_params=_params("parallel", "parallel", "arbitrary"),
        name="mla_attn",
    )(q, k, v)


def _positions(lp, s_real):
    r = jnp.arange(lp, dtype=jnp.int32)
    return jnp.where(r < s_real, r + N_META, r - s_real).astype(F32)


def _rope_angles(pos, dim):
    inv_freq = 1.0 / (ROPE_THETA ** (jnp.arange(0, dim, 2, dtype=F32) / dim))
    ang = pos[:, None] * inv_freq[None, :]
    return jnp.cos(ang), jnp.sin(ang)


def _swa_tables(pos):
    cos, sin = _rope_angles(pos, A_HEAD_DIM)
    cos_t = jnp.tile(cos, (1, 4))
    sin_t = jnp.tile(jnp.concatenate([-sin, sin], axis=1), (1, 2))
    return cos_t, sin_t


def _mla_tables(pos, scale):
    cos, sin = _rope_angles(pos, B_ROPE)
    n = pos.shape[0]
    z16 = jnp.zeros((n, B_ROPE // 2), F32)
    tail = jnp.zeros((n, 128 - B_NOPE - B_ROPE), F32)
    ones = jnp.ones((n, B_NOPE), F32)
    zeros = jnp.zeros((n, B_NOPE), F32)
    cos_t = jnp.concatenate([ones, cos, cos, tail], axis=1) * scale
    sin_a = jnp.concatenate([zeros, -sin, z16, tail], axis=1) * scale
    sin_b = jnp.concatenate([zeros, z16, sin, tail], axis=1) * scale
    return cos_t, sin_a, sin_b


def _pad_heads(w, n_heads, width, pad_to):
    k = w.shape[0]
    w = w.reshape(k, n_heads, width)
    w = jnp.pad(w, ((0, 0), (0, 0), (0, pad_to - width)))
    return w.reshape(k, n_heads * pad_to)


def kernel(x, meta_tokens, attn_norm, ffn_norm, final_norm, a_w_qkv, a_b_qkv, a_sinks, a_w_o, a_b_o,
           b_w_in, b_q_norm, b_kv_norm, b_w_uq, b_w_ukv, b_w_o,
           moe_w_group, moe_b_group, moe_w_router, moe_b_router, moe_w_gate, moe_w_up, moe_w_down):
    batch, s_real, _ = x.shape
    lp = s_real + BLOCK
    tp = batch * lp
    assert tp % ROW_TILE == 0 and lp % ROW_TILE == 0 and s_real % BLOCK == 0

    meta = jnp.broadcast_to(meta_tokens[None].astype(x.dtype), (batch, N_META, D_MODEL))
    pad = jnp.zeros((batch, BLOCK - N_META, D_MODEL), x.dtype)
    h = jnp.concatenate([x, meta, pad], axis=1).reshape(tp, D_MODEL)
    pos = _positions(lp, s_real)

    def router_params(i):
        wr = jnp.concatenate([moe_w_router[i], moe_w_group[i]], axis=1)
        wr = jnp.pad(wr, ((0, 0), (0, ROUTE_LANES - wr.shape[1])))
        br = jnp.concatenate([moe_b_router[i], moe_b_group[i]])
        br = jnp.pad(br, (0, ROUTE_LANES - br.shape[0]))[None]
        return wr, br

    def moe(i, h1, route, route_t, final):
        return _moe_layer(h1, route, route_t, ffn_norm[i][None], moe_w_gate[i].astype(BF16), moe_w_up[i].astype(BF16),
                          moe_w_down[i].astype(BF16), final_norm[None], batch, lp, s_real, final)

    wq, wk, wv = a_w_qkv[0][:, :A_Q_W], a_w_qkv[0][:, A_Q_W:A_Q_W + A_KV_W], a_w_qkv[0][:, A_Q_W + A_KV_W:]
    bq, bk, bv = a_b_qkv[0][:A_Q_W], a_b_qkv[0][A_Q_W:A_Q_W + A_KV_W], a_b_qkv[0][A_Q_W + A_KV_W:]
    swap = lambda t: jnp.concatenate([t[..., A_HEAD_DIM:], t[..., :A_HEAD_DIM]], axis=-1)
    w_a = jnp.concatenate([wq, wk, swap(wk), wv, swap(wv)], axis=1).astype(BF16)
    b_a = jnp.concatenate([bq, bk, swap(bk), bv, swap(bv)])[None]
    cos_a, sin_a = _swa_tables(pos)
    q, kk, vv = _swa_qkv(h, attn_norm[0][None], w_a, b_a, cos_a, sin_a, lp)
    sink_rows = jnp.repeat(a_sinks[0].astype(F32).reshape(A_KV_HEADS, 4, 2).transpose(0, 2, 1).reshape(4, 4), BLOCK, axis=1)[..., None]
    o = _swa_attn(q, kk, vv, sink_rows, batch, lp, s_real)
    wr, br = router_params(0)
    h1, route, route_t = _attn_out(o, a_w_o[0].astype(BF16), a_b_o[0][None], h, ffn_norm[0][None], wr, br)
    h = moe(0, h1, route, route_t, False).reshape(tp, D_MODEL)

    scale = (B_NOPE + B_ROPE) ** -0.5
    w_in = b_w_in[0]
    kpe_cols = jnp.pad(w_in[:, B_Q_RANK + B_KV_RANK:], ((0, 0), (B_NOPE, 128 - B_NOPE - B_ROPE)))
    w_in_p = jnp.concatenate([w_in[:, :B_Q_RANK + B_KV_RANK], kpe_cols], axis=1).astype(BF16)
    w_q_p = _pad_heads(b_w_uq[0], B_HEADS, B_NOPE + B_ROPE, B_HEAD_PAD).astype(BF16)
    w_ukv = b_w_ukv[0].reshape(B_KV_RANK, B_HEADS, B_NOPE + B_V)
    w_k_p = _pad_heads(w_ukv[:, :, :B_NOPE].reshape(B_KV_RANK, -1), B_HEADS, B_NOPE, B_HEAD_PAD).astype(BF16)
    w_v_p = _pad_heads(w_ukv[:, :, B_NOPE:].reshape(B_KV_RANK, -1), B_HEADS, B_V, B_HEAD_PAD).astype(BF16)
    tabs = _mla_tables(pos, scale * math.log2(math.e)) + _mla_tables(pos, 1.0)
    q, k, v = _mla_proj(h, attn_norm[1][None], w_in_p, b_q_norm[0][None], b_kv_norm[0][None],
                        w_q_p, w_k_p, w_v_p, tabs, batch, lp)
    o = _mla_attn(q, k, v, batch, lp, s_real).reshape(tp, B_HEADS * B_HEAD_PAD)
    w_o_p = jnp.pad(b_w_o[0].reshape(B_HEADS, B_V, D_MODEL), ((0, 0), (0, B_HEAD_PAD - B_V), (0, 0)))
    w_o_p = w_o_p.reshape(B_HEADS * B_HEAD_PAD, D_MODEL).astype(BF16)
    wr, br = router_params(1)
    h1, route, route_t = _attn_out(o, w_o_p, jnp.zeros((1, D_MODEL), F32), h, ffn_norm[1][None], wr, br)
    return moe(1, h1, route, route_t, True)
```

```python
import functools
import math

import jax
import jax.numpy as jnp
from jax import lax
from jax.experimental import pallas as pl
from jax.experimental.pallas import tpu as pltpu

F32 = jnp.float32
BF16 = jnp.bfloat16

D_MODEL = 1024
N_META = 16
BLOCK = 128
ROPE_THETA = 10000.0
NORM_EPS = 1e-6
NEG_INF = -1e30

A_HEADS = 16
A_KV_HEADS = 2
A_HEAD_DIM = 64
A_Q_W = A_HEADS * A_HEAD_DIM
A_KV_W = A_KV_HEADS * A_HEAD_DIM

B_HEADS = 16
B_NOPE = 64
B_ROPE = 32
B_V = 64
B_Q_RANK = 256
B_KV_RANK = 128
B_HEAD_PAD = 128

N_GROUPS = 4
EXPERTS_PER_GROUP = 8
N_EXPERTS = N_GROUPS * EXPERTS_PER_GROUP
TOP_K = 2
D_EXPERT = 256
MOE_BLOCK = 128
ROUTE_LANES = 128

ROW_TILE = 640
DMA_UNROLL = 8
MLA_Q_TILE = 1024
MLA_SUB_ROWS = 256
VMEM_LIMIT = 56 * 1024 * 1024


def _rms(x, g):
    return x * lax.rsqrt(jnp.mean(x * x, axis=-1, keepdims=True) + NORM_EPS) * g


def _params(*sem):
    return pltpu.CompilerParams(dimension_semantics=sem, vmem_limit_bytes=VMEM_LIMIT)


def _swa_qkv_kernel(h_ref, g_ref, w_ref, b_ref, cos_ref, sin_ref, q_ref, k_ref, v_ref):
    xn = _rms(h_ref[...], g_ref[...]).astype(BF16)
    y = jnp.dot(xn, w_ref[...], preferred_element_type=F32) + b_ref[...]
    cos = cos_ref[...]
    sin = sin_ref[...]
    lane = lax.broadcasted_iota(jnp.int32, cos.shape, 1)
    first = (lane & (A_HEAD_DIM - 1)) < (A_HEAD_DIM // 2)

    def rope(c):
        rot = jnp.where(first, pltpu.roll(c, 128 - A_HEAD_DIM // 2, 1), pltpu.roll(c, A_HEAD_DIM // 2, 1))
        return c * cos + rot * sin

    for j in range(A_Q_W // 128):
        q_ref[:, j * 128:(j + 1) * 128] = (rope(y[:, j * 128:(j + 1) * 128]) * (A_HEAD_DIM ** -0.5)).astype(BF16)
    k_ref[:, 0:128] = rope(y[:, A_Q_W:A_Q_W + 128]).astype(BF16)
    k_ref[:, 128:256] = rope(y[:, A_Q_W + 128:A_Q_W + 256]).astype(BF16)
    v_ref[...] = y[:, A_Q_W + 256:A_Q_W + 512].astype(BF16)


def _swa_qkv(h, g, w, b, cos, sin, lp):
    tp = h.shape[0]
    tiles_per_batch = lp // ROW_TILE
    n_out = w.shape[1]
    row = lambda i: (i, 0)
    fixed = lambda i: (0, 0)
    tab = lambda i: (i % tiles_per_batch, 0)
    return pl.pallas_call(
        _swa_qkv_kernel,
        grid=(tp // ROW_TILE,),
        in_specs=[
            pl.BlockSpec((ROW_TILE, D_MODEL), row),
            pl.BlockSpec((1, D_MODEL), fixed),
            pl.BlockSpec((D_MODEL, n_out), fixed),
            pl.BlockSpec((1, n_out), fixed),
            pl.BlockSpec((ROW_TILE, 128), tab),
            pl.BlockSpec((ROW_TILE, 128), tab),
        ],
        out_specs=[
            pl.BlockSpec((ROW_TILE, A_Q_W), row),
            pl.BlockSpec((ROW_TILE, 256), row),
            pl.BlockSpec((ROW_TILE, 256), row),
        ],
        out_shape=[
            jax.ShapeDtypeStruct((tp, A_Q_W), BF16),
            jax.ShapeDtypeStruct((tp, 256), BF16),
            jax.ShapeDtypeStruct((tp, 256), BF16),
        ],
        compiler_params=_params("parallel"),
        name="swa_qkv",
    )(h, g, w, b, cos, sin)


def _swa_attn_kernel(q_ref, kc_ref, kp_ref, km_ref, vc_ref, vp_ref, vm_ref, sink_ref, o_ref, *, n_real_blocks):
    n = pl.program_id(1)
    is_real = n < n_real_blocks
    n_keys = N_META + 2 * BLOCK
    big = jnp.int32(1 << 20)
    key = lax.broadcasted_iota(jnp.int32, (n_keys, BLOCK), 0)
    qry = lax.broadcasted_iota(jnp.int32, (n_keys, BLOCK), 1)
    jp = key - N_META
    jc = key - (N_META + BLOCK)
    meta_lim = jnp.where(is_real, N_META, 0)
    prev_off = jnp.where(jnp.logical_and(is_real, n >= 1), 0, big)
    cur_lim = jnp.where(is_real, big, N_META)
    ok = (key < meta_lim) | ((jc < 0) & (jp > qry + prev_off)) | ((jc >= 0) & (jc <= qry) & (jc < cur_lim))
    bias1 = jnp.where(ok, 0.0, NEG_INF).astype(F32)
    bias = jnp.concatenate([bias1] * 4, axis=1)

    kcat = jnp.concatenate([km_ref[0], kp_ref[0], kc_ref[0]], axis=0)
    vcat = jnp.concatenate([vm_ref[0], vp_ref[0], vc_ref[0]], axis=0)
    lo = lax.broadcasted_iota(jnp.int32, (n_keys, 128), 1) < A_HEAD_DIM
    zero = jnp.zeros((n_keys, 128), BF16)

    def scores(kvh, parity):
        a, b = (kcat[:, :128], kcat[:, 128:]) if kvh == 0 else (kcat[:, 128:], kcat[:, :128])
        kx = jnp.where(lo, a, zero) if parity == 0 else jnp.where(lo, zero, b)
        base = kvh * (A_Q_W // A_KV_HEADS)
        qs = jnp.concatenate([q_ref[0, :, base + c * 128:base + (c + 1) * 128] for c in range(4)], axis=0)
        st = lax.dot_general(kx, qs, (((1,), (1,)), ((), ())), preferred_element_type=F32) + bias

        def finish():
            a, b = (vcat[:, :128], vcat[:, 128:]) if kvh == 0 else (vcat[:, 128:], vcat[:, :128])
            vx = jnp.where(lo, a, zero) if parity == 0 else jnp.where(lo, zero, b)
            sink = sink_ref[kvh * 2 + parity]
            m = jnp.maximum(jnp.max(st, axis=0, keepdims=True), sink)
            p = jnp.exp(st - m)
            denom = jnp.sum(p, axis=0, keepdims=True) + jnp.exp(sink - m)
            ot = lax.dot_general(vx, p.astype(BF16), (((0,), (0,)), ((), ())), preferred_element_type=F32)
            return ot * (1.0 / denom)
        return finish

    for kvh in range(A_KV_HEADS):
        base = kvh * (A_Q_W // A_KV_HEADS)
        even = scores(kvh, 0)
        odd = scores(kvh, 1)
        ot = even() + odd()
        for c in range(4):
            o_ref[0, :, base + c * 128:base + (c + 1) * 128] = ot[:, c * BLOCK:(c + 1) * BLOCK].T.astype(BF16)


def _swa_attn(q, kk, vv, sink_rows, batch, lp, s_real):
    nb = s_real // BLOCK
    q3 = q.reshape(batch, lp, A_Q_W)
    k3 = kk.reshape(batch, lp, 256)
    v3 = vv.reshape(batch, lp, 256)
    cur = lambda b, n: (b, n, 0)
    prev = lambda b, n: (b, jnp.maximum(n - 1, 0), 0)
    meta = lambda b, n: (b, s_real // N_META, 0)
    out = pl.pallas_call(
        functools.partial(_swa_attn_kernel, n_real_blocks=nb),
        grid=(batch, nb + 1),
        in_specs=[
            pl.BlockSpec((1, BLOCK, A_Q_W), cur),
            pl.BlockSpec((1, BLOCK, 256), cur),
            pl.BlockSpec((1, BLOCK, 256), prev),
            pl.BlockSpec((1, N_META, 256), meta),
            pl.BlockSpec((1, BLOCK, 256), cur),
            pl.BlockSpec((1, BLOCK, 256), prev),
            pl.BlockSpec((1, N_META, 256), meta),
            pl.BlockSpec((4, 1, 4 * BLOCK), lambda b, n: (0, 0, 0)),
        ],
        out_specs=pl.BlockSpec((1, BLOCK, A_Q_W), cur),
        out_shape=jax.ShapeDtypeStruct((batch, lp, A_Q_W), BF16),
        compiler_params=_params("parallel", "parallel"),
        name="swa_attn",
    )(q3, k3, k3, k3, v3, v3, v3, sink_rows)
    return out.reshape(batch * lp, A_Q_W)


ROUTE_ROWS = 40


def _route(xn, wr_ref, br):
    x_hi = xn.astype(BF16)
    x_lo = (xn - x_hi.astype(F32)).astype(BF16)
    both = jnp.dot(x_hi, wr_ref[...], preferred_element_type=F32)
    lg = (both[:, :ROUTE_LANES] + both[:, ROUTE_LANES:]
          + jnp.dot(x_lo, wr_ref[:, :ROUTE_LANES], preferred_element_type=F32) + br)
    lt = lg.T[0:ROUTE_ROWS]
    row_i = lax.broadcasted_iota(jnp.int32, lt.shape, 0)
    row = row_i.astype(F32)
    row_grp = (row_i >> 3).astype(F32)
    big = 1e9
    is_g = jnp.logical_and(row_i >= N_EXPERTS, row_i < N_EXPERTS + N_GROUPS)
    gl = jnp.where(is_g, lt, -jnp.inf)
    gmax = jnp.max(gl, axis=0, keepdims=True)
    g_p = 1.0 / jnp.sum(jnp.exp(gl - gmax), axis=0, keepdims=True)
    g_idx = jnp.min(jnp.where(gl == gmax, row - float(N_EXPERTS), big), axis=0, keepdims=True)
    el = jnp.where(row_grp == g_idx, lt, -jnp.inf)
    m1 = jnp.max(el, axis=0, keepdims=True)
    esum = jnp.sum(jnp.exp(el - m1), axis=0, keepdims=True)
    i1 = jnp.min(jnp.where(el == m1, row, big), axis=0, keepdims=True)
    el2 = jnp.where(row == i1, -jnp.inf, el)
    m2 = jnp.max(el2, axis=0, keepdims=True)
    i2 = jnp.min(jnp.where(el2 == m2, row, big), axis=0, keepdims=True)
    e1 = 1.0 / esum
    e2 = jnp.exp(m2 - m1) / esum
    w1 = g_p * e1 / (e1 + e2)
    w2 = g_p * e2 / (e1 + e2)
    return jnp.concatenate([i1, i2, w1, w2, jnp.zeros((4, lt.shape[1]), F32)], axis=0)


def _attn_out_kernel(o_ref, w_ref, b_ref, h_ref, g_ref, wr_ref, br_ref, h1_ref, route_t_ref):
    h1 = h_ref[...] + jnp.dot(o_ref[...], w_ref[...], preferred_element_type=F32) + b_ref[...]
    h1_ref[...] = h1
    route_t_ref[...] = _route(_rms(h1, g_ref[...]), wr_ref, br_ref[...])


def _attn_out(o, w, b, h, g, wr, br):
    tp, ko = o.shape
    row = lambda i: (i, 0)
    fixed = lambda i: (0, 0)
    return pl.pallas_call(
        _attn_out_kernel,
        grid=(tp // ROW_TILE,),
        in_specs=[
            pl.BlockSpec((ROW_TILE, ko), row),
            pl.BlockSpec((ko, D_MODEL), fixed),
            pl.BlockSpec((1, D_MODEL), fixed),
            pl.BlockSpec((ROW_TILE, D_MODEL), row),
            pl.BlockSpec((1, D_MODEL), fixed),
            pl.BlockSpec((D_MODEL, 2 * ROUTE_LANES), fixed),
            pl.BlockSpec((1, ROUTE_LANES), fixed),
        ],
        out_specs=[
            pl.BlockSpec((ROW_TILE, D_MODEL), row),
            pl.BlockSpec((8, ROW_TILE), lambda i: (0, i)),
        ],
        out_shape=[
            jax.ShapeDtypeStruct((tp, D_MODEL), F32),
            jax.ShapeDtypeStruct((8, tp), F32),
        ],
        compiler_params=_params("parallel"),
        name="attn_out_route",
    )(o, w, b, h, g, wr, br)


def _plan_kernel(rt_ref, pos_ref, be_ref, pends_ref, cnt_col, carry, *, n_blocks_pad):
    phase = pl.program_id(0)
    i = pl.program_id(1)
    tm = rt_ref.shape[1]
    id0 = rt_ref[0:1, :]
    id1 = rt_ref[1:2, :]
    e_sub = lax.broadcasted_iota(jnp.int32, (N_EXPERTS, tm), 0).astype(F32)
    hit0 = e_sub == id0
    hit1 = e_sub == id1
    member_t = jnp.where(hit0, 1.0, jnp.where(hit1, 1.0, 0.0))
    tile_cnt_col = jnp.sum(member_t, axis=1, keepdims=True)

    @pl.when(jnp.logical_and(phase == 0, i == 0))
    def _():
        cnt_col[...] = jnp.zeros(cnt_col.shape, F32)

    @pl.when(phase == 0)
    def _():
        cnt_col[...] += jnp.broadcast_to(tile_cnt_col, cnt_col.shape)

    @pl.when(jnp.logical_and(phase == 1, i == 0))
    def _():
        pad_to_block = lambda c: jnp.floor((c + (MOE_BLOCK - 1.0)) * (1.0 / MOE_BLOCK)) * MOE_BLOCK
        padded_col = pad_to_block(cnt_col[...])
        square = jnp.concatenate([padded_col, jnp.zeros((128 - N_EXPERTS, 128), F32)], axis=0)
        padded_row = square.T[0:1]
        lane_e = lax.broadcasted_iota(jnp.int32, (N_EXPERTS, 128), 1)
        sub_e = lax.broadcasted_iota(jnp.int32, (N_EXPERTS, 128), 0)
        pstart_col = jnp.sum(jnp.where(lane_e < sub_e, padded_row, 0.0), axis=1, keepdims=True)
        pends_row = jnp.sum(jnp.where(sub_e <= lane_e, padded_col, 0.0), axis=0, keepdims=True)
        pends_col = pstart_col + padded_col[:, 0:1]
        carry[...] = jnp.broadcast_to(pstart_col, carry.shape)
        pends_ref[...] = pends_row.astype(jnp.int32)
        blk_lane = lax.broadcasted_iota(jnp.int32, (N_EXPERTS, n_blocks_pad), 1)
        blk_start = (blk_lane * MOE_BLOCK).astype(F32)
        be = jnp.sum(jnp.where(pends_col <= blk_start, 1.0, 0.0), axis=0, keepdims=True)
        be = jnp.minimum(be, N_EXPERTS - 1.0)
        n_used = pends_row[:, N_EXPERTS - 1:N_EXPERTS] * (1.0 / MOE_BLOCK)
        be = jnp.where(blk_lane[0:1] == n_blocks_pad - 1, n_used, be)
        be_ref[...] = be.astype(jnp.int32)

    @pl.when(phase == 1)
    def _():
        before = lax.broadcasted_iota(jnp.int32, (tm, tm), 0) < lax.broadcasted_iota(jnp.int32, (tm, tm), 1)
        prefix = jnp.dot(member_t.astype(BF16), jnp.where(before, 1.0, 0.0).astype(BF16),
                         preferred_element_type=F32)
        row_of = prefix + carry[:, 0:1]
        dest0 = jnp.sum(jnp.where(hit0, row_of, 0.0), axis=0, keepdims=True)
        dest1 = jnp.sum(jnp.where(hit1, row_of, 0.0), axis=0, keepdims=True)
        pos_ref[...] = jnp.concatenate([dest0, dest1], axis=0).astype(jnp.int32)
        carry[...] += jnp.broadcast_to(tile_cnt_col, carry.shape)


def _dispatch_plan(route_t, n_blocks):
    tp = route_t.shape[1]
    n_tiles = tp // ROW_TILE
    n_blocks_pad = -(-(n_blocks + 1) // 128) * 128
    pos, be, pends = pl.pallas_call(
        functools.partial(_plan_kernel, n_blocks_pad=n_blocks_pad),
        grid=(2, n_tiles),
        in_specs=[
            pl.BlockSpec((8, ROW_TILE), lambda p, i: (0, i)),
        ],
        out_specs=[
            pl.BlockSpec((TOP_K, ROW_TILE), lambda p, i: (0, i * p)),
            pl.BlockSpec((1, n_blocks_pad), lambda p, i: (0, 0)),
            pl.BlockSpec((1, 128), lambda p, i: (0, 0)),
        ],
        out_shape=[
            jax.ShapeDtypeStruct((TOP_K, tp), jnp.int32),
            jax.ShapeDtypeStruct((1, n_blocks_pad), jnp.int32),
            jax.ShapeDtypeStruct((1, 128), jnp.int32),
        ],
        scratch_shapes=[pltpu.VMEM((N_EXPERTS, 128), F32), pltpu.VMEM((N_EXPERTS, 128), F32)],
        compiler_params=_params("arbitrary", "arbitrary"),
        name="moe_plan",
    )(route_t)
    return pos.reshape(-1), be.reshape(-1), pends.reshape(-1)


def _scatter_row(stage, slot, r, buf_hbm, dst_row, sem):
    return pltpu.make_async_copy(stage.at[slot, pl.ds(r, 1)], buf_hbm.at[pl.ds(dst_row, 1)], sem.at[slot])


def _dispatch_kernel(pos_ref, pends_ref, h_ref, buf_hbm, stage, zsem, sem, *, n_tokens, n_blocks):
    i = pl.program_id(0)
    slot = i % 2

    def zero_block(row0):
        return pltpu.make_async_copy(stage.at[0], buf_hbm.at[pl.ds(pl.multiple_of(row0, MOE_BLOCK), MOE_BLOCK)], zsem)

    def zero_tail(e):
        return zero_block(jnp.maximum(pends_ref[e] - MOE_BLOCK, 0))

    @pl.when(i == 0)
    def _():
        stage[0] = jnp.zeros((BLOCK, D_MODEL), F32)
        for e in range(N_EXPERTS):
            zero_tail(e).start()
        for e in range(N_EXPERTS):
            zero_tail(e).wait()
        first_unused = pends_ref[N_EXPERTS - 1] // MOE_BLOCK

        def start_unused(b, carry):
            zero_block(b * MOE_BLOCK).start()
            return carry

        def wait_unused(b, carry):
            zero_block(b * MOE_BLOCK).wait()
            return carry
        lax.fori_loop(first_unused, n_blocks, start_unused, 0)
        lax.fori_loop(first_unused, n_blocks, wait_unused, 0)

    stage[slot] = h_ref[...]

    def issue(c, carry):
        for u in range(DMA_UNROLL):
            r = c * DMA_UNROLL + u
            t = i * BLOCK + r
            _scatter_row(stage, slot, r, buf_hbm, pos_ref[t], sem).start()
            _scatter_row(stage, slot, r, buf_hbm, pos_ref[n_tokens + t], sem).start()
        return carry
    lax.fori_loop(0, BLOCK // DMA_UNROLL, issue, 0)

    def drain(which):
        for _ in range(2 * BLOCK):
            _scatter_row(stage, which, 0, buf_hbm, 0, sem).wait()

    @pl.when(i > 0)
    def _():
        drain(1 - slot)

    @pl.when(i == pl.num_programs(0) - 1)
    def _():
        drain(slot)


def _moe_dispatch(h1, pos, pends, n_rows):
    tp = h1.shape[0]
    grid_spec = pltpu.PrefetchScalarGridSpec(
        num_scalar_prefetch=2,
        grid=(tp // BLOCK,),
        in_specs=[pl.BlockSpec((BLOCK, D_MODEL), lambda i, pos, pends: (i, 0))],
        out_specs=pl.BlockSpec(memory_space=pl.ANY),
        scratch_shapes=[
            pltpu.VMEM((2, BLOCK, D_MODEL), F32),
            pltpu.SemaphoreType.DMA(()),
            pltpu.SemaphoreType.DMA((2,)),
        ],
    )
    return pl.pallas_call(
        functools.partial(_dispatch_kernel, n_tokens=tp, n_blocks=n_rows // MOE_BLOCK),
        grid_spec=grid_spec,
        out_shape=jax.ShapeDtypeStruct((n_rows, D_MODEL), F32),
        compiler_params=_params("arbitrary"),
        name="moe_dispatch",
    )(pos, pends, h1)


def _moe_kernel(be_ref, x_ref, g_ref, wga_ref, wua_ref, wda_ref, wgb_ref, wub_ref, wdb_ref, y_ref,
                wg_bf, wu_bf, wd_bf, *, n_blocks_pad):
    i = pl.program_id(0)
    n_used = be_ref[n_blocks_pad - 1]
    weights = ((wga_ref, wua_ref, wda_ref), (wgb_ref, wub_ref, wdb_ref))

    def expert_of(blk):
        return be_ref[jnp.clip(blk, 0, n_used - 1)]

    for half, (wg_ref, wu_ref, wd_ref) in enumerate(weights):
        blk = 2 * i + half

        @pl.when(jnp.logical_or(i == 0, expert_of(blk) != expert_of(blk - 2)))
        def _():
            wg_bf[half] = wg_ref[0].astype(BF16)
            wu_bf[half] = wu_ref[0].astype(BF16)
            wd_bf[half] = wd_ref[0].astype(BF16)

    def rows(half):
        return slice(half * MOE_BLOCK, (half + 1) * MOE_BLOCK)

    def gate_up(half):
        xn = _rms(x_ref[rows(half), :], g_ref[...]).astype(BF16)
        return (jnp.dot(xn, wg_bf[half], preferred_element_type=F32),
                jnp.dot(xn, wu_bf[half], preferred_element_type=F32))

    def down(half, gate, up):
        act = (gate * jax.nn.sigmoid(gate) * up).astype(BF16)
        y_ref[rows(half), :] = jnp.dot(act, wd_bf[half], preferred_element_type=F32)

    def zero(half):
        y_ref[rows(half), :] = jnp.zeros((MOE_BLOCK, D_MODEL), F32)

    used_a = 2 * i < n_used
    used_b = 2 * i + 1 < n_used

    @pl.when(used_b)
    def _():
        ga, ua = gate_up(0)
        gb, ub = gate_up(1)
        down(0, ga, ua)
        down(1, gb, ub)

    @pl.when(jnp.logical_and(used_a, jnp.logical_not(used_b)))
    def _():
        down(0, *gate_up(0))
        zero(1)

    @pl.when(jnp.logical_not(used_a))
    def _():
        zero(0)
        zero(1)


def _moe_experts(buf, g, wg, wu, wd, block_expert, n_blocks):
    n_blocks_pad = block_expert.shape[0]
    assert n_blocks % 2 == 0
    last = lambda be: be[n_blocks_pad - 1] - 1
    pair = lambda i, be: (jnp.minimum(i, last(be) // 2), 0)
    ew = lambda half: (lambda i, be: (be[jnp.minimum(2 * i + half, last(be))], 0, 0))
    weights = lambda half: [
        pl.BlockSpec((1, D_MODEL, D_EXPERT), ew(half)),
        pl.BlockSpec((1, D_MODEL, D_EXPERT), ew(half)),
        pl.BlockSpec((1, D_EXPERT, D_MODEL), ew(half)),
    ]
    grid_spec = pltpu.PrefetchScalarGridSpec(
        num_scalar_prefetch=1,
        grid=(n_blocks // 2,),
        in_specs=[
            pl.BlockSpec((2 * MOE_BLOCK, D_MODEL), pair),
            pl.BlockSpec((1, D_MODEL), lambda i, be: (0, 0)),
        ] + weights(0) + weights(1),
        out_specs=pl.BlockSpec((2 * MOE_BLOCK, D_MODEL), lambda i, be: (i, 0)),
        scratch_shapes=[
            pltpu.VMEM((2, D_MODEL, D_EXPERT), BF16),
            pltpu.VMEM((2, D_MODEL, D_EXPERT), BF16),
            pltpu.VMEM((2, D_EXPERT, D_MODEL), BF16),
        ],
    )
    return pl.pallas_call(
        functools.partial(_moe_kernel, n_blocks_pad=n_blocks_pad),
        grid_spec=grid_spec,
        out_shape=jax.ShapeDtypeStruct((n_blocks * MOE_BLOCK, D_MODEL), F32),
        compiler_params=_params("arbitrary"),
        name="moe_experts",
    )(block_expert, buf, g, wg, wu, wd, wg, wu, wd)


def _row_copy(src_hbm, src_row, dst, slot, dst_row, sem):
    return pltpu.make_async_copy(src_hbm.at[pl.ds(src_row, 1)], dst.at[slot, pl.ds(dst_row, 1)], sem.at[slot])


def _combine_kernel(pos_ref, h_ref, rt_ref, g_ref, y_hbm, o_ref, buf, sem, *, tiles_per_batch, n_tokens, final):
    b = pl.program_id(0)
    j = pl.program_id(1)
    nj = pl.num_programs(1)
    step = b * nj + j
    slot = step % 2

    def issue(bb, jj, to_slot):
        tile = bb * tiles_per_batch + jj
        def body(c, carry):
            for u in range(DMA_UNROLL):
                r = c * DMA_UNROLL + u
                t = tile * BLOCK + r
                _row_copy(y_hbm, pos_ref[t], buf, to_slot, r, sem).start()
                _row_copy(y_hbm, pos_ref[n_tokens + t], buf, to_slot, BLOCK + r, sem).start()
            return carry
        lax.fori_loop(0, BLOCK // DMA_UNROLL, body, 0)

    @pl.when(step == 0)
    def _():
        issue(b, j, 0)

    @pl.when(step + 1 < pl.num_programs(0) * nj)
    def _():
        nxt = j + 1
        wrap = nxt == nj
        issue(jnp.where(wrap, b + 1, b), jnp.where(wrap, 0, nxt), 1 - slot)

    for _ in range(2 * BLOCK):
        _row_copy(y_hbm, 0, buf, slot, 0, sem).wait()

    w = jnp.concatenate([rt_ref[...], jnp.zeros((BLOCK - 8, BLOCK), F32)], axis=0).T
    h2 = h_ref[0] + w[:, TOP_K:TOP_K + 1] * buf[slot, 0:BLOCK] + w[:, TOP_K + 1:TOP_K + 2] * buf[slot, BLOCK:2 * BLOCK]
    if final:
        h2 = _rms(h2, g_ref[...])
    o_ref[0] = h2


def _moe_combine(h1, route_t, y, pos, g, batch, lp, s_real, final):
    tiles_per_batch = lp // BLOCK
    nj = s_real // BLOCK if final else tiles_per_batch
    out_rows = s_real if final else lp
    tile = lambda b, j, pos: (b, j, 0)
    grid_spec = pltpu.PrefetchScalarGridSpec(
        num_scalar_prefetch=1,
        grid=(batch, nj),
        in_specs=[
            pl.BlockSpec((1, BLOCK, D_MODEL), tile),
            pl.BlockSpec((8, BLOCK), lambda b, j, pos: (0, b * tiles_per_batch + j)),
            pl.BlockSpec((1, D_MODEL), lambda b, j, pos: (0, 0)),
            pl.BlockSpec(memory_space=pl.ANY),
        ],
        out_specs=pl.BlockSpec((1, BLOCK, D_MODEL), tile),
        scratch_shapes=[
            pltpu.VMEM((2, 2 * BLOCK, D_MODEL), F32),
            pltpu.SemaphoreType.DMA((2,)),
        ],
    )
    return pl.pallas_call(
        functools.partial(_combine_kernel, tiles_per_batch=tiles_per_batch, n_tokens=batch * lp, final=final),
        grid_spec=grid_spec,
        out_shape=jax.ShapeDtypeStruct((batch, out_rows, D_MODEL), F32),
        compiler_params=_params("arbitrary", "arbitrary"),
        name="moe_combine_final" if final else "moe_combine",
    )(pos, h1.reshape(batch, lp, D_MODEL), route_t, g, y)


def _moe_layer(h1, route_t, g, wg, wu, wd, final_g, batch, lp, s_real, final):
    tp = batch * lp
    n_blocks = -(-(tp * TOP_K + N_EXPERTS * (MOE_BLOCK - 1)) // MOE_BLOCK)
    n_blocks += n_blocks % 2
    pos, block_expert, pends = _dispatch_plan(route_t, n_blocks)
    buf = _moe_dispatch(h1, pos, pends, n_blocks * MOE_BLOCK)
    y = _moe_experts(buf, g, wg, wu, wd, block_expert, n_blocks)
    return _moe_combine(h1, route_t, y, pos, final_g, batch, lp, s_real, final)


def _mla_rope(c, cos, sin_a, sin_b):
    half = B_ROPE // 2
    return c * cos + pltpu.roll(c, 128 - half, 1) * sin_a + pltpu.roll(c, half, 1) * sin_b


def _mla_proj_kernel(h_ref, g_ref, win_ref, qn_ref, kvn_ref, wq_ref, wk_ref, wv_ref,
                     qc_ref, qsa_ref, qsb_ref, kc_ref, ksa_ref, ksb_ref, q_ref, k_ref, v_ref):
    xn = _rms(h_ref[...], g_ref[...]).astype(BF16)
    c = jnp.dot(xn, win_ref[...], preferred_element_type=F32)
    cq = _rms(c[:, :B_Q_RANK], qn_ref[...]).astype(BF16)
    ckv = _rms(c[:, B_Q_RANK:B_Q_RANK + B_KV_RANK], kvn_ref[...]).astype(BF16)
    kpe = _mla_rope(c[:, B_Q_RANK + B_KV_RANK:], kc_ref[...], ksa_ref[...], ksb_ref[...])
    q = jnp.dot(cq, wq_ref[...], preferred_element_type=F32)
    k = jnp.dot(ckv, wk_ref[...], preferred_element_type=F32)
    v = jnp.dot(ckv, wv_ref[...], preferred_element_type=F32)
    qc, qsa, qsb = qc_ref[...], qsa_ref[...], qsb_ref[...]
    ones_col = (lax.broadcasted_iota(jnp.int32, kpe.shape, 1) == B_V).astype(F32)
    for hd in range(B_HEADS):
        sl = slice(hd * B_HEAD_PAD, (hd + 1) * B_HEAD_PAD)
        q_ref[0, hd] = _mla_rope(q[:, sl], qc, qsa, qsb).astype(BF16)
        k_ref[0, hd] = (k[:, sl] + kpe).astype(BF16)
        v_ref[0, hd] = (v[:, sl] + ones_col).astype(BF16)


def _mla_proj(h, g, win, qn, kvn, wq, wk, wv, tabs, batch, lp):
    tiles_per_batch = lp // ROW_TILE
    row = lambda b, i: (b * tiles_per_batch + i, 0)
    fixed = lambda b, i: (0, 0)
    tab = lambda b, i: (i, 0)
    head_out = pl.BlockSpec((1, B_HEADS, ROW_TILE, B_HEAD_PAD), lambda b, i: (b, 0, i, 0))
    hw = B_HEADS * B_HEAD_PAD
    shape = jax.ShapeDtypeStruct((batch, B_HEADS, lp, B_HEAD_PAD), BF16)
    return pl.pallas_call(
        _mla_proj_kernel,
        grid=(batch, tiles_per_batch),
        in_specs=[
            pl.BlockSpec((ROW_TILE, D_MODEL), row),
            pl.BlockSpec((1, D_MODEL), fixed),
            pl.BlockSpec((D_MODEL, 512), fixed),
            pl.BlockSpec((1, B_Q_RANK), fixed),
            pl.BlockSpec((1, B_KV_RANK), fixed),
            pl.BlockSpec((B_Q_RANK, hw), fixed),
            pl.BlockSpec((B_KV_RANK, hw), fixed),
            pl.BlockSpec((B_KV_RANK, hw), fixed),
        ] + [pl.BlockSpec((ROW_TILE, 128), tab)] * 6,
        out_specs=[head_out, head_out, head_out],
        out_shape=[shape, shape, shape],
        compiler_params=_params("parallel", "parallel"),
        name="mla_proj",
    )(h, g, win, qn, kvn, wq, wk, wv, *tabs)


def _mla_attn_kernel(q_ref, k_ref, v_ref, o_ref, m_sc, acc_sc, *, s_real, tq, tk):
    qi = pl.program_id(2)
    nq = s_real // tq
    is_meta_q = qi == nq
    nt = (((1,), (1,)), ((), ()))
    sub = min(MLA_SUB_ROWS, tq)
    n_sub = tq // sub

    m_sc[...] = jnp.full(m_sc.shape, NEG_INF, F32)
    acc_sc[...] = jnp.zeros(acc_sc.shape, F32)

    def item(r, key0, width, mask_fn, with_meta=False):
        rows = slice(r * sub, (r + 1) * sub)
        k = k_ref[0, 0, pl.ds(key0, width), :]
        if with_meta:
            k = jnp.concatenate([k_ref[0, 0, s_real:s_real + BLOCK, :], k], axis=0)
        s = lax.dot_general(q_ref[0, 0, rows, :], k, nt, preferred_element_type=F32)
        if mask_fn is not None:
            rr = lax.broadcasted_iota(jnp.int32, s.shape, 0) + r * sub
            cc = lax.broadcasted_iota(jnp.int32, s.shape, 1)
            s = jnp.where(mask_fn(rr, cc), s, NEG_INF)

        def finish():
            v = v_ref[0, 0, pl.ds(key0, width), :]
            if with_meta:
                v = jnp.concatenate([v_ref[0, 0, s_real:s_real + BLOCK, :], v], axis=0)
            m_old = m_sc[rows]
            m_new = jnp.maximum(m_old, jnp.max(s, axis=-1, keepdims=True))
            p = jnp.exp2(s - jnp.concatenate([m_new] * (s.shape[1] // 128), axis=1)).astype(BF16)
            acc_sc[rows] = acc_sc[rows] * jnp.exp2(m_old - m_new) + jnp.dot(p, v, preferred_element_type=F32)
            m_sc[rows] = m_new
        return finish

    def run(items):
        pending = None
        for make in items:
            nxt = make()
            if pending is not None:
                pending()
            pending = nxt
        pending()

    def full_body(j, carry):
        start = pl.multiple_of(j * tk, tk)
        run([functools.partial(item, r, start, tk, None) for r in range(n_sub)])
        return carry

    n_full = jnp.where(is_meta_q, 0, qi * (tq // tk))
    lax.fori_loop(0, n_full, full_body, 0)

    @pl.when(jnp.logical_not(is_meta_q))
    def _():
        base = pl.multiple_of(qi * tq, tq)
        diag_mask = lambda rr, cc: jnp.where(cc < BLOCK, cc, cc - BLOCK) <= jnp.where(cc < BLOCK, N_META - 1, rr)
        run([functools.partial(item, r, base, (r + 1) * sub, diag_mask, True) for r in range(n_sub)])
        acc = acc_sc[...]
        o_ref[0] = (acc * (1.0 / acc[:, B_V:B_V + 1])).astype(BF16)

    @pl.when(is_meta_q)
    def _():
        rows = min(sub, BLOCK)
        item(0, s_real, BLOCK, lambda rr, cc: (cc <= rr) & (cc < N_META))()
        acc = acc_sc[0:rows]
        o_ref[0, 0:rows] = (acc * (1.0 / acc[:, B_V:B_V + 1])).astype(BF16)


def _mla_attn(q, k, v, batch, lp, s_real):
    tq = min(MLA_Q_TILE, s_real)
    tk = tq
    nq = s_real // tq
    qmap = lambda b, h, i: (b, h, i, 0)
    kvmap = lambda b, h, i: (b, h, 0, 0)
    return pl.pallas_call(
        functools.partial(_mla_attn_kernel, s_real=s_real, tq=tq, tk=tk),
        grid=(batch, B_HEADS, nq + 1),
        in_specs=[
            pl.BlockSpec((1, 1, tq, B_HEAD_PAD), qmap),
            pl.BlockSpec((1, 1, lp, B_HEAD_PAD), kvmap),
            pl.BlockSpec((1, 1, lp, B_HEAD_PAD), kvmap),
        ],
        out_specs=pl.BlockSpec((1, tq, B_HEAD_PAD), lambda b, h, i: (b, i, h)),
        out_shape=jax.ShapeDtypeStruct((batch, lp, B_HEADS * B_HEAD_PAD), BF16),
        scratch_shapes=[pltpu.VMEM((tq, 128), F32), pltpu.VMEM((tq, B_HEAD_PAD), F32)],
        compiler_params=_params("parallel", "parallel", "arbitrary"),
        name="mla_attn",
    )(q, k, v)


def _positions(lp, s_real):
    r = jnp.arange(lp, dtype=jnp.int32)
    return jnp.where(r < s_real, r + N_META, r - s_real).astype(F32)


def _rope_angles(pos, dim):
    inv_freq = 1.0 / (ROPE_THETA ** (jnp.arange(0, dim, 2, dtype=F32) / dim))
    ang = pos[:, None] * inv_freq[None, :]
    return jnp.cos(ang), jnp.sin(ang)


def _swa_tables(pos):
    cos, sin = _rope_angles(pos, A_HEAD_DIM)
    cos_t = jnp.tile(cos, (1, 4))
    sin_t = jnp.tile(jnp.concatenate([-sin, sin], axis=1), (1, 2))
    return cos_t, sin_t


def _mla_tables(pos, scale):
    cos, sin = _rope_angles(pos, B_ROPE)
    n = pos.shape[0]
    z16 = jnp.zeros((n, B_ROPE // 2), F32)
    tail = jnp.zeros((n, 128 - B_NOPE - B_ROPE), F32)
    ones = jnp.ones((n, B_NOPE), F32)
    zeros = jnp.zeros((n, B_NOPE), F32)
    cos_t = jnp.concatenate([ones, cos, cos, tail], axis=1) * scale
    sin_a = jnp.concatenate([zeros, -sin, z16, tail], axis=1) * scale
    sin_b = jnp.concatenate([zeros, z16, sin, tail], axis=1) * scale
    return cos_t, sin_a, sin_b


def _pad_heads(w, n_heads, width, pad_to):
    k = w.shape[0]
    w = w.reshape(k, n_heads, width)
    w = jnp.pad(w, ((0, 0), (0, 0), (0, pad_to - width)))
    return w.reshape(k, n_heads * pad_to)


def kernel(x, meta_tokens, attn_norm, ffn_norm, final_norm, a_w_qkv, a_b_qkv, a_sinks, a_w_o, a_b_o,
           b_w_in, b_q_norm, b_kv_norm, b_w_uq, b_w_ukv, b_w_o,
           moe_w_group, moe_b_group, moe_w_router, moe_b_router, moe_w_gate, moe_w_up, moe_w_down):
    batch, s_real, _ = x.shape
    lp = s_real + BLOCK
    tp = batch * lp
    assert tp % ROW_TILE == 0 and lp % ROW_TILE == 0 and s_real % BLOCK == 0

    meta = jnp.broadcast_to(meta_tokens[None].astype(x.dtype), (batch, N_META, D_MODEL))
    pad = jnp.zeros((batch, BLOCK - N_META, D_MODEL), x.dtype)
    h = jnp.concatenate([x, meta, pad], axis=1).reshape(tp, D_MODEL)
    pos = _positions(lp, s_real)

    def router_params(i):
        wr = jnp.concatenate([moe_w_router[i], moe_w_group[i]], axis=1)
        wr = jnp.pad(wr, ((0, 0), (0, ROUTE_LANES - wr.shape[1])))
        br = jnp.concatenate([moe_b_router[i], moe_b_group[i]])
        br = jnp.pad(br, (0, ROUTE_LANES - br.shape[0]))[None]
        w_hi = wr.astype(BF16)
        w_lo = (wr - w_hi.astype(F32)).astype(BF16)
        return jnp.concatenate([w_hi, w_lo], axis=1), br

    def moe(i, h1, route_t, final):
        return _moe_layer(h1, route_t, ffn_norm[i][None], moe_w_gate[i], moe_w_up[i], moe_w_down[i],
                          final_norm[None], batch, lp, s_real, final)

    wq, wk, wv = a_w_qkv[0][:, :A_Q_W], a_w_qkv[0][:, A_Q_W:A_Q_W + A_KV_W], a_w_qkv[0][:, A_Q_W + A_KV_W:]
    bq, bk, bv = a_b_qkv[0][:A_Q_W], a_b_qkv[0][A_Q_W:A_Q_W + A_KV_W], a_b_qkv[0][A_Q_W + A_KV_W:]
    swap = lambda t: jnp.concatenate([t[..., A_HEAD_DIM:], t[..., :A_HEAD_DIM]], axis=-1)
    w_a = jnp.concatenate([wq, wk, swap(wk), wv, swap(wv)], axis=1).astype(BF16)
    b_a = jnp.concatenate([bq, bk, swap(bk), bv, swap(bv)])[None]
    cos_a, sin_a = _swa_tables(pos)
    q, kk, vv = _swa_qkv(h, attn_norm[0][None], w_a, b_a, cos_a, sin_a, lp)
    sink_rows = jnp.repeat(a_sinks[0].astype(F32).reshape(A_KV_HEADS, 4, 2).transpose(0, 2, 1).reshape(4, 4), BLOCK, axis=1)[:, None, :]
    o = _swa_attn(q, kk, vv, sink_rows, batch, lp, s_real)
    wr, br = router_params(0)
    h1, route_t = _attn_out(o, a_w_o[0].astype(BF16), a_b_o[0][None], h, ffn_norm[0][None], wr, br)
    h = moe(0, h1, route_t, False).reshape(tp, D_MODEL)

    scale = (B_NOPE + B_ROPE) ** -0.5
    w_in = b_w_in[0]
    kpe_cols = jnp.pad(w_in[:, B_Q_RANK + B_KV_RANK:], ((0, 0), (B_NOPE, 128 - B_NOPE - B_ROPE)))
    w_in_p = jnp.concatenate([w_in[:, :B_Q_RANK + B_KV_RANK], kpe_cols], axis=1).astype(BF16)
    w_q_p = _pad_heads(b_w_uq[0], B_HEADS, B_NOPE + B_ROPE, B_HEAD_PAD).astype(BF16)
    w_ukv = b_w_ukv[0].reshape(B_KV_RANK, B_HEADS, B_NOPE + B_V)
    w_k_p = _pad_heads(w_ukv[:, :, :B_NOPE].reshape(B_KV_RANK, -1), B_HEADS, B_NOPE, B_HEAD_PAD).astype(BF16)
    w_v_p = _pad_heads(w_ukv[:, :, B_NOPE:].reshape(B_KV_RANK, -1), B_HEADS, B_V, B_HEAD_PAD).astype(BF16)
    tabs = _mla_tables(pos, scale * math.log2(math.e)) + _mla_tables(pos, 1.0)
    q, k, v = _mla_proj(h, attn_norm[1][None], w_in_p, b_q_norm[0][None], b_kv_norm[0][None],
                        w_q_p, w_k_p, w_v_p, tabs, batch, lp)
    o = _mla_attn(q, k, v, batch, lp, s_real).reshape(tp, B_HEADS * B_HEAD_PAD)
    w_o_p = jnp.pad(b_w_o[0].reshape(B_HEADS, B_V, D_MODEL), ((0, 0), (0, B_HEAD_PAD - B_V), (0, 0)))
    w_o_p = w_o_p.reshape(B_HEADS * B_HEAD_PAD, D_MODEL).astype(BF16)
    wr, br = router_params(1)
    h1, route_t = _attn_out(o, w_o_p, jnp.zeros((1, D_MODEL), F32), h, ffn_norm[1][None], wr, br)
    return moe(1, h1, route_t, True)
```

```python
import functools
import math

import jax
import jax.numpy as jnp
from jax import lax
from jax.experimental import pallas as pl
from jax.experimental.pallas import tpu as pltpu

F32 = jnp.float32
BF16 = jnp.bfloat16

D_MODEL = 1024
N_META = 16
BLOCK = 128
ROPE_THETA = 10000.0
NORM_EPS = 1e-6
NEG_INF = -1e30

A_HEADS = 16
A_KV_HEADS = 2
A_HEAD_DIM = 64
A_Q_W = A_HEADS * A_HEAD_DIM
A_KV_W = A_KV_HEADS * A_HEAD_DIM

B_HEADS = 16
B_NOPE = 64
B_ROPE = 32
B_V = 64
B_Q_RANK = 256
B_KV_RANK = 128
B_HEAD_PAD = 128

N_GROUPS = 4
EXPERTS_PER_GROUP = 8
N_EXPERTS = N_GROUPS * EXPERTS_PER_GROUP
TOP_K = 2
D_EXPERT = 256
MOE_BLOCK = 128
ROUTE_LANES = 128

ROW_TILE = 640
DMA_UNROLL = 8
MLA_Q_TILE = 2048
MLA_K_CHUNK = 1024
MLA_SUB_ROWS = 256
VMEM_LIMIT = 56 * 1024 * 1024


def _rms(x, g):
    return x * lax.rsqrt(jnp.mean(x * x, axis=-1, keepdims=True) + NORM_EPS) * g


def _params(*sem):
    return pltpu.CompilerParams(dimension_semantics=sem, vmem_limit_bytes=VMEM_LIMIT)


def _swa_qkv_kernel(h_ref, g_ref, w_ref, b_ref, cos_ref, sin_ref, q_ref, k_ref, v_ref):
    xn = _rms(h_ref[...], g_ref[...]).astype(BF16)
    y = jnp.dot(xn, w_ref[...], preferred_element_type=F32) + b_ref[...]
    cos = cos_ref[...]
    sin = sin_ref[...]
    lane = lax.broadcasted_iota(jnp.int32, cos.shape, 1)
    first = (lane & (A_HEAD_DIM - 1)) < (A_HEAD_DIM // 2)

    def rope(c):
        rot = jnp.where(first, pltpu.roll(c, 128 - A_HEAD_DIM // 2, 1), pltpu.roll(c, A_HEAD_DIM // 2, 1))
        return c * cos + rot * sin

    for j in range(A_Q_W // 128):
        q_ref[:, j * 128:(j + 1) * 128] = (rope(y[:, j * 128:(j + 1) * 128]) * (A_HEAD_DIM ** -0.5)).astype(BF16)
    k_ref[:, 0:128] = rope(y[:, A_Q_W:A_Q_W + 128]).astype(BF16)
    k_ref[:, 128:256] = rope(y[:, A_Q_W + 128:A_Q_W + 256]).astype(BF16)
    v_ref[...] = y[:, A_Q_W + 256:A_Q_W + 512].astype(BF16)


def _swa_qkv(h, g, w, b, cos, sin, lp):
    tp = h.shape[0]
    tiles_per_batch = lp // ROW_TILE
    n_out = w.shape[1]
    row = lambda i: (i, 0)
    fixed = lambda i: (0, 0)
    tab = lambda i: (i % tiles_per_batch, 0)
    return pl.pallas_call(
        _swa_qkv_kernel,
        grid=(tp // ROW_TILE,),
        in_specs=[
            pl.BlockSpec((ROW_TILE, D_MODEL), row),
            pl.BlockSpec((1, D_MODEL), fixed),
            pl.BlockSpec((D_MODEL, n_out), fixed),
            pl.BlockSpec((1, n_out), fixed),
            pl.BlockSpec((ROW_TILE, 128), tab),
            pl.BlockSpec((ROW_TILE, 128), tab),
        ],
        out_specs=[
            pl.BlockSpec((ROW_TILE, A_Q_W), row),
            pl.BlockSpec((ROW_TILE, 256), row),
            pl.BlockSpec((ROW_TILE, 256), row),
        ],
        out_shape=[
            jax.ShapeDtypeStruct((tp, A_Q_W), BF16),
            jax.ShapeDtypeStruct((tp, 256), BF16),
            jax.ShapeDtypeStruct((tp, 256), BF16),
        ],
        compiler_params=_params("parallel"),
        name="swa_qkv",
    )(h, g, w, b, cos, sin)


def _swa_attn_kernel(q_ref, kc_ref, kp_ref, km_ref, vc_ref, vp_ref, vm_ref, sink_ref, o_ref, *, n_real_blocks):
    n = pl.program_id(1)
    is_real = n < n_real_blocks
    n_keys = N_META + 2 * BLOCK
    big = jnp.int32(1 << 20)
    key = lax.broadcasted_iota(jnp.int32, (n_keys, BLOCK), 0)
    qry = lax.broadcasted_iota(jnp.int32, (n_keys, BLOCK), 1)
    jp = key - N_META
    jc = key - (N_META + BLOCK)
    meta_lim = jnp.where(is_real, N_META, 0)
    prev_off = jnp.where(jnp.logical_and(is_real, n >= 1), 0, big)
    cur_lim = jnp.where(is_real, big, N_META)
    ok = (key < meta_lim) | ((jc < 0) & (jp > qry + prev_off)) | ((jc >= 0) & (jc <= qry) & (jc < cur_lim))
    bias1 = jnp.where(ok, 0.0, NEG_INF).astype(F32)
    bias = jnp.concatenate([bias1] * 4, axis=1)

    kcat = jnp.concatenate([km_ref[0], kp_ref[0], kc_ref[0]], axis=0)
    vcat = jnp.concatenate([vm_ref[0], vp_ref[0], vc_ref[0]], axis=0)
    lo = lax.broadcasted_iota(jnp.int32, (n_keys, 128), 1) < A_HEAD_DIM
    zero = jnp.zeros((n_keys, 128), BF16)

    def scores(kvh, parity):
        a, b = (kcat[:, :128], kcat[:, 128:]) if kvh == 0 else (kcat[:, 128:], kcat[:, :128])
        kx = jnp.where(lo, a, zero) if parity == 0 else jnp.where(lo, zero, b)
        base = kvh * (A_Q_W // A_KV_HEADS)
        qs = jnp.concatenate([q_ref[0, :, base + c * 128:base + (c + 1) * 128] for c in range(4)], axis=0)
        st = lax.dot_general(kx, qs, (((1,), (1,)), ((), ())), preferred_element_type=F32) + bias

        def finish():
            a, b = (vcat[:, :128], vcat[:, 128:]) if kvh == 0 else (vcat[:, 128:], vcat[:, :128])
            vx = jnp.where(lo, a, zero) if parity == 0 else jnp.where(lo, zero, b)
            sink = sink_ref[kvh * 2 + parity]
            m = jnp.maximum(jnp.max(st, axis=0, keepdims=True), sink)
            p = jnp.exp(st - m)
            denom = jnp.sum(p, axis=0, keepdims=True) + jnp.exp(sink - m)
            ot = lax.dot_general(vx, p.astype(BF16), (((0,), (0,)), ((), ())), preferred_element_type=F32)
            return ot * (1.0 / denom)
        return finish

    for kvh in range(A_KV_HEADS):
        base = kvh * (A_Q_W // A_KV_HEADS)
        even = scores(kvh, 0)
        odd = scores(kvh, 1)
        ot = even() + odd()
        for c in range(4):
            o_ref[0, :, base + c * 128:base + (c + 1) * 128] = ot[:, c * BLOCK:(c + 1) * BLOCK].T.astype(BF16)


def _swa_attn(q, kk, vv, sink_rows, batch, lp, s_real):
    nb = s_real // BLOCK
    q3 = q.reshape(batch, lp, A_Q_W)
    k3 = kk.reshape(batch, lp, 256)
    v3 = vv.reshape(batch, lp, 256)
    cur = lambda b, n: (b, n, 0)
    prev = lambda b, n: (b, jnp.maximum(n - 1, 0), 0)
    meta = lambda b, n: (b, s_real // N_META, 0)
    out = pl.pallas_call(
        functools.partial(_swa_attn_kernel, n_real_blocks=nb),
        grid=(batch, nb + 1),
        in_specs=[
            pl.BlockSpec((1, BLOCK, A_Q_W), cur),
            pl.BlockSpec((1, BLOCK, 256), cur),
            pl.BlockSpec((1, BLOCK, 256), prev),
            pl.BlockSpec((1, N_META, 256), meta),
            pl.BlockSpec((1, BLOCK, 256), cur),
            pl.BlockSpec((1, BLOCK, 256), prev),
            pl.BlockSpec((1, N_META, 256), meta),
            pl.BlockSpec((4, 1, 4 * BLOCK), lambda b, n: (0, 0, 0)),
        ],
        out_specs=pl.BlockSpec((1, BLOCK, A_Q_W), cur),
        out_shape=jax.ShapeDtypeStruct((batch, lp, A_Q_W), BF16),
        compiler_params=_params("parallel", "parallel"),
        name="swa_attn",
    )(q3, k3, k3, k3, v3, v3, v3, sink_rows)
    return out.reshape(batch * lp, A_Q_W)


ROUTE_ROWS = 40


def _route(xn, wr_ref, br):
    x_hi = xn.astype(BF16)
    x_lo = (xn - x_hi.astype(F32)).astype(BF16)
    both = jnp.dot(x_hi, wr_ref[...], preferred_element_type=F32)
    lg = (both[:, :ROUTE_LANES] + both[:, ROUTE_LANES:]
          + jnp.dot(x_lo, wr_ref[:, :ROUTE_LANES], preferred_element_type=F32) + br)
    lt = lg.T[0:ROUTE_ROWS]
    row_i = lax.broadcasted_iota(jnp.int32, lt.shape, 0)
    row = row_i.astype(F32)
    row_grp = (row_i >> 3).astype(F32)
    big = 1e9
    is_g = jnp.logical_and(row_i >= N_EXPERTS, row_i < N_EXPERTS + N_GROUPS)
    gl = jnp.where(is_g, lt, -jnp.inf)
    gmax = jnp.max(gl, axis=0, keepdims=True)
    g_p = 1.0 / jnp.sum(jnp.exp(gl - gmax), axis=0, keepdims=True)
    g_idx = jnp.min(jnp.where(gl == gmax, row - float(N_EXPERTS), big), axis=0, keepdims=True)
    el = jnp.where(row_grp == g_idx, lt, -jnp.inf)
    m1 = jnp.max(el, axis=0, keepdims=True)
    esum = jnp.sum(jnp.exp(el - m1), axis=0, keepdims=True)
    i1 = jnp.min(jnp.where(el == m1, row, big), axis=0, keepdims=True)
    el2 = jnp.where(row == i1, -jnp.inf, el)
    m2 = jnp.max(el2, axis=0, keepdims=True)
    i2 = jnp.min(jnp.where(el2 == m2, row, big), axis=0, keepdims=True)
    e1 = 1.0 / esum
    e2 = jnp.exp(m2 - m1) / esum
    w1 = g_p * e1 / (e1 + e2)
    w2 = g_p * e2 / (e1 + e2)
    return jnp.concatenate([i1, i2, w1, w2, jnp.zeros((4, lt.shape[1]), F32)], axis=0)


def _attn_out_kernel(o_ref, w_ref, b_ref, h_ref, g_ref, wr_ref, br_ref, h1_ref, route_t_ref):
    h1 = h_ref[...] + jnp.dot(o_ref[...], w_ref[...], preferred_element_type=F32) + b_ref[...]
    h1_ref[...] = h1
    route_t_ref[...] = _route(_rms(h1, g_ref[...]), wr_ref, br_ref[...])


def _attn_out(o, w, b, h, g, wr, br):
    tp, ko = o.shape
    row = lambda i: (i, 0)
    fixed = lambda i: (0, 0)
    return pl.pallas_call(
        _attn_out_kernel,
        grid=(tp // ROW_TILE,),
        in_specs=[
            pl.BlockSpec((ROW_TILE, ko), row),
            pl.BlockSpec((ko, D_MODEL), fixed),
            pl.BlockSpec((1, D_MODEL), fixed),
            pl.BlockSpec((ROW_TILE, D_MODEL), row),
            pl.BlockSpec((1, D_MODEL), fixed),
            pl.BlockSpec((D_MODEL, 2 * ROUTE_LANES), fixed),
            pl.BlockSpec((1, ROUTE_LANES), fixed),
        ],
        out_specs=[
            pl.BlockSpec((ROW_TILE, D_MODEL), row),
            pl.BlockSpec((8, ROW_TILE), lambda i: (0, i)),
        ],
        out_shape=[
            jax.ShapeDtypeStruct((tp, D_MODEL), F32),
            jax.ShapeDtypeStruct((8, tp), F32),
        ],
        compiler_params=_params("parallel"),
        name="attn_out_route",
    )(o, w, b, h, g, wr, br)


def _plan_kernel(rt_ref, pos_ref, be_ref, pends_ref, cnt_col, carry, *, n_blocks_pad):
    phase = pl.program_id(0)
    i = pl.program_id(1)
    tm = rt_ref.shape[1]
    id0 = rt_ref[0:1, :]
    id1 = rt_ref[1:2, :]
    e_sub = lax.broadcasted_iota(jnp.int32, (N_EXPERTS, tm), 0).astype(F32)
    hit0 = e_sub == id0
    hit1 = e_sub == id1
    member_t = jnp.where(hit0, 1.0, jnp.where(hit1, 1.0, 0.0))
    tile_cnt_col = jnp.sum(member_t, axis=1, keepdims=True)

    @pl.when(jnp.logical_and(phase == 0, i == 0))
    def _():
        cnt_col[...] = jnp.zeros(cnt_col.shape, F32)

    @pl.when(phase == 0)
    def _():
        cnt_col[...] += jnp.broadcast_to(tile_cnt_col, cnt_col.shape)

    @pl.when(jnp.logical_and(phase == 1, i == 0))
    def _():
        pad_to_block = lambda c: jnp.floor((c + (MOE_BLOCK - 1.0)) * (1.0 / MOE_BLOCK)) * MOE_BLOCK
        padded_col = pad_to_block(cnt_col[...])
        square = jnp.concatenate([padded_col, jnp.zeros((128 - N_EXPERTS, 128), F32)], axis=0)
        padded_row = square.T[0:1]
        lane_e = lax.broadcasted_iota(jnp.int32, (N_EXPERTS, 128), 1)
        sub_e = lax.broadcasted_iota(jnp.int32, (N_EXPERTS, 128), 0)
        pstart_col = jnp.sum(jnp.where(lane_e < sub_e, padded_row, 0.0), axis=1, keepdims=True)
        pends_row = jnp.sum(jnp.where(sub_e <= lane_e, padded_col, 0.0), axis=0, keepdims=True)
        pends_col = pstart_col + padded_col[:, 0:1]
        carry[...] = jnp.broadcast_to(pstart_col, carry.shape)
        pends_ref[...] = pends_row.astype(jnp.int32)
        blk_lane = lax.broadcasted_iota(jnp.int32, (N_EXPERTS, n_blocks_pad), 1)
        blk_start = (blk_lane * MOE_BLOCK).astype(F32)
        be = jnp.sum(jnp.where(pends_col <= blk_start, 1.0, 0.0), axis=0, keepdims=True)
        be = jnp.minimum(be, N_EXPERTS - 1.0)
        n_used = pends_row[:, N_EXPERTS - 1:N_EXPERTS] * (1.0 / MOE_BLOCK)
        be = jnp.where(blk_lane[0:1] == n_blocks_pad - 1, n_used, be)
        be_ref[...] = be.astype(jnp.int32)

    @pl.when(phase == 1)
    def _():
        before = lax.broadcasted_iota(jnp.int32, (tm, tm), 0) < lax.broadcasted_iota(jnp.int32, (tm, tm), 1)
        prefix = jnp.dot(member_t.astype(BF16), jnp.where(before, 1.0, 0.0).astype(BF16),
                         preferred_element_type=F32)
        row_of = prefix + carry[:, 0:1]
        dest0 = jnp.sum(jnp.where(hit0, row_of, 0.0), axis=0, keepdims=True)
        dest1 = jnp.sum(jnp.where(hit1, row_of, 0.0), axis=0, keepdims=True)
        pos_ref[...] = jnp.concatenate([dest0, dest1], axis=0).astype(jnp.int32)
        carry[...] += jnp.broadcast_to(tile_cnt_col, carry.shape)


def _dispatch_plan(route_t, n_blocks):
    tp = route_t.shape[1]
    n_tiles = tp // ROW_TILE
    n_blocks_pad = -(-(n_blocks + 1) // 128) * 128
    pos, be, pends = pl.pallas_call(
        functools.partial(_plan_kernel, n_blocks_pad=n_blocks_pad),
        grid=(2, n_tiles),
        in_specs=[
            pl.BlockSpec((8, ROW_TILE), lambda p, i: (0, i)),
        ],
        out_specs=[
            pl.BlockSpec((TOP_K, ROW_TILE), lambda p, i: (0, i * p)),
            pl.BlockSpec((1, n_blocks_pad), lambda p, i: (0, 0)),
            pl.BlockSpec((1, 128), lambda p, i: (0, 0)),
        ],
        out_shape=[
            jax.ShapeDtypeStruct((TOP_K, tp), jnp.int32),
            jax.ShapeDtypeStruct((1, n_blocks_pad), jnp.int32),
            jax.ShapeDtypeStruct((1, 128), jnp.int32),
        ],
        scratch_shapes=[pltpu.VMEM((N_EXPERTS, 128), F32), pltpu.VMEM((N_EXPERTS, 128), F32)],
        compiler_params=_params("arbitrary", "arbitrary"),
        name="moe_plan",
    )(route_t)
    return pos.reshape(-1), be.reshape(-1), pends.reshape(-1)


def _scatter_row(stage, slot, r, buf_hbm, dst_row, sem):
    return pltpu.make_async_copy(stage.at[slot, pl.ds(r, 1)], buf_hbm.at[pl.ds(dst_row, 1)], sem.at[slot])


def _dispatch_kernel(pos_ref, pends_ref, h_ref, buf_hbm, stage, zsem, sem, *, n_tokens, n_blocks):
    i = pl.program_id(0)
    slot = i % 2

    def zero_block(row0):
        return pltpu.make_async_copy(stage.at[0], buf_hbm.at[pl.ds(pl.multiple_of(row0, MOE_BLOCK), MOE_BLOCK)], zsem)

    def zero_tail(e):
        return zero_block(jnp.maximum(pends_ref[e] - MOE_BLOCK, 0))

    @pl.when(i == 0)
    def _():
        stage[0] = jnp.zeros((BLOCK, D_MODEL), F32)
        for e in range(N_EXPERTS):
            zero_tail(e).start()
        for e in range(N_EXPERTS):
            zero_tail(e).wait()
        first_unused = pends_ref[N_EXPERTS - 1] // MOE_BLOCK

        def start_unused(b, carry):
            zero_block(b * MOE_BLOCK).start()
            return carry

        def wait_unused(b, carry):
            zero_block(b * MOE_BLOCK).wait()
            return carry
        lax.fori_loop(first_unused, n_blocks, start_unused, 0)
        lax.fori_loop(first_unused, n_blocks, wait_unused, 0)

    stage[slot] = h_ref[...]

    def issue(c, carry):
        for u in range(DMA_UNROLL):
            r = c * DMA_UNROLL + u
            t = i * BLOCK + r
            _scatter_row(stage, slot, r, buf_hbm, pos_ref[t], sem).start()
            _scatter_row(stage, slot, r, buf_hbm, pos_ref[n_tokens + t], sem).start()
        return carry
    lax.fori_loop(0, BLOCK // DMA_UNROLL, issue, 0)

    def drain(which):
        for _ in range(2 * BLOCK):
            _scatter_row(stage, which, 0, buf_hbm, 0, sem).wait()

    @pl.when(i > 0)
    def _():
        drain(1 - slot)

    @pl.when(i == pl.num_programs(0) - 1)
    def _():
        drain(slot)


def _moe_dispatch(h1, pos, pends, n_rows):
    tp = h1.shape[0]
    grid_spec = pltpu.PrefetchScalarGridSpec(
        num_scalar_prefetch=2,
        grid=(tp // BLOCK,),
        in_specs=[pl.BlockSpec((BLOCK, D_MODEL), lambda i, pos, pends: (i, 0))],
        out_specs=pl.BlockSpec(memory_space=pl.ANY),
        scratch_shapes=[
            pltpu.VMEM((2, BLOCK, D_MODEL), F32),
            pltpu.SemaphoreType.DMA(()),
            pltpu.SemaphoreType.DMA((2,)),
        ],
    )
    return pl.pallas_call(
        functools.partial(_dispatch_kernel, n_tokens=tp, n_blocks=n_rows // MOE_BLOCK),
        grid_spec=grid_spec,
        out_shape=jax.ShapeDtypeStruct((n_rows, D_MODEL), F32),
        compiler_params=_params("arbitrary"),
        name="moe_dispatch",
    )(pos, pends, h1)


def _moe_kernel(be_ref, x_ref, g_ref, wga_ref, wua_ref, wda_ref, wgb_ref, wub_ref, wdb_ref, y_ref,
                wg_bf, wu_bf, wd_bf, *, n_blocks_pad):
    i = pl.program_id(0)
    n_used = be_ref[n_blocks_pad - 1]
    weights = ((wga_ref, wua_ref, wda_ref), (wgb_ref, wub_ref, wdb_ref))

    def expert_of(blk):
        return be_ref[jnp.clip(blk, 0, n_used - 1)]

    for half, (wg_ref, wu_ref, wd_ref) in enumerate(weights):
        blk = 2 * i + half

        @pl.when(jnp.logical_or(i == 0, expert_of(blk) != expert_of(blk - 2)))
        def _():
            wg_bf[half] = wg_ref[0].astype(BF16)
            wu_bf[half] = wu_ref[0].astype(BF16)
            wd_bf[half] = wd_ref[0].astype(BF16)

    def rows(half):
        return slice(half * MOE_BLOCK, (half + 1) * MOE_BLOCK)

    def gate_up(half):
        xn = _rms(x_ref[rows(half), :], g_ref[...]).astype(BF16)
        return (jnp.dot(xn, wg_bf[half], preferred_element_type=F32),
                jnp.dot(xn, wu_bf[half], preferred_element_type=F32))

    def down(half, gate, up):
        act = (gate * jax.nn.sigmoid(gate) * up).astype(BF16)
        y_ref[rows(half), :] = jnp.dot(act, wd_bf[half], preferred_element_type=F32)

    def zero(half):
        y_ref[rows(half), :] = jnp.zeros((MOE_BLOCK, D_MODEL), F32)

    used_a = 2 * i < n_used
    used_b = 2 * i + 1 < n_used

    @pl.when(used_b)
    def _():
        ga, ua = gate_up(0)
        gb, ub = gate_up(1)
        down(0, ga, ua)
        down(1, gb, ub)

    @pl.when(jnp.logical_and(used_a, jnp.logical_not(used_b)))
    def _():
        down(0, *gate_up(0))
        zero(1)

    @pl.when(jnp.logical_not(used_a))
    def _():
        zero(0)
        zero(1)


def _moe_experts(buf, g, wg, wu, wd, layer, block_expert, n_blocks):
    n_blocks_pad = block_expert.shape[0]
    assert n_blocks % 2 == 0
    last = lambda be: be[n_blocks_pad - 1] - 1
    pair = lambda i, be: (jnp.minimum(i, last(be) // 2), 0)
    ew = lambda half: (lambda i, be: (layer * N_EXPERTS + be[jnp.minimum(2 * i + half, last(be))], 0, 0))
    weights = lambda half: [
        pl.BlockSpec((1, D_MODEL, D_EXPERT), ew(half)),
        pl.BlockSpec((1, D_MODEL, D_EXPERT), ew(half)),
        pl.BlockSpec((1, D_EXPERT, D_MODEL), ew(half)),
    ]
    grid_spec = pltpu.PrefetchScalarGridSpec(
        num_scalar_prefetch=1,
        grid=(n_blocks // 2,),
        in_specs=[
            pl.BlockSpec((2 * MOE_BLOCK, D_MODEL), pair),
            pl.BlockSpec((1, D_MODEL), lambda i, be: (0, 0)),
        ] + weights(0) + weights(1),
        out_specs=pl.BlockSpec((2 * MOE_BLOCK, D_MODEL), lambda i, be: (i, 0)),
        scratch_shapes=[
            pltpu.VMEM((2, D_MODEL, D_EXPERT), BF16),
            pltpu.VMEM((2, D_MODEL, D_EXPERT), BF16),
            pltpu.VMEM((2, D_EXPERT, D_MODEL), BF16),
        ],
    )
    return pl.pallas_call(
        functools.partial(_moe_kernel, n_blocks_pad=n_blocks_pad),
        grid_spec=grid_spec,
        out_shape=jax.ShapeDtypeStruct((n_blocks * MOE_BLOCK, D_MODEL), F32),
        compiler_params=_params("arbitrary"),
        name="moe_experts",
    )(block_expert, buf, g, wg, wu, wd, wg, wu, wd)


def _row_copy(src_hbm, src_row, dst, slot, dst_row, sem):
    return pltpu.make_async_copy(src_hbm.at[pl.ds(src_row, 1)], dst.at[slot, pl.ds(dst_row, 1)], sem.at[slot])


def _combine_kernel(pos_ref, h_ref, rt_ref, g_ref, y_hbm, o_ref, buf, sem, *, tiles_per_batch, n_tokens, final):
    b = pl.program_id(0)
    j = pl.program_id(1)
    nj = pl.num_programs(1)
    step = b * nj + j
    slot = step % 2

    def issue(bb, jj, to_slot):
        tile = bb * tiles_per_batch + jj
        def body(c, carry):
            for u in range(DMA_UNROLL):
                r = c * DMA_UNROLL + u
                t = tile * BLOCK + r
                _row_copy(y_hbm, pos_ref[t], buf, to_slot, r, sem).start()
                _row_copy(y_hbm, pos_ref[n_tokens + t], buf, to_slot, BLOCK + r, sem).start()
            return carry
        lax.fori_loop(0, BLOCK // DMA_UNROLL, body, 0)

    @pl.when(step == 0)
    def _():
        issue(b, j, 0)

    @pl.when(step + 1 < pl.num_programs(0) * nj)
    def _():
        nxt = j + 1
        wrap = nxt == nj
        issue(jnp.where(wrap, b + 1, b), jnp.where(wrap, 0, nxt), 1 - slot)

    for _ in range(2 * BLOCK):
        _row_copy(y_hbm, 0, buf, slot, 0, sem).wait()

    w = jnp.concatenate([rt_ref[...], jnp.zeros((BLOCK - 8, BLOCK), F32)], axis=0).T
    h2 = h_ref[0] + w[:, TOP_K:TOP_K + 1] * buf[slot, 0:BLOCK] + w[:, TOP_K + 1:TOP_K + 2] * buf[slot, BLOCK:2 * BLOCK]
    if final:
        h2 = _rms(h2, g_ref[...])
    o_ref[0] = h2


def _moe_combine(h1, route_t, y, pos, g, batch, lp, s_real, final):
    tiles_per_batch = lp // BLOCK
    nj = s_real // BLOCK if final else tiles_per_batch
    out_rows = s_real if final else lp
    tile = lambda b, j, pos: (b, j, 0)
    grid_spec = pltpu.PrefetchScalarGridSpec(
        num_scalar_prefetch=1,
        grid=(batch, nj),
        in_specs=[
            pl.BlockSpec((1, BLOCK, D_MODEL), tile),
            pl.BlockSpec((8, BLOCK), lambda b, j, pos: (0, b * tiles_per_batch + j)),
            pl.BlockSpec((1, D_MODEL), lambda b, j, pos: (0, 0)),
            pl.BlockSpec(memory_space=pl.ANY),
        ],
        out_specs=pl.BlockSpec((1, BLOCK, D_MODEL), tile),
        scratch_shapes=[
            pltpu.VMEM((2, 2 * BLOCK, D_MODEL), F32),
            pltpu.SemaphoreType.DMA((2,)),
        ],
    )
    return pl.pallas_call(
        functools.partial(_combine_kernel, tiles_per_batch=tiles_per_batch, n_tokens=batch * lp, final=final),
        grid_spec=grid_spec,
        out_shape=jax.ShapeDtypeStruct((batch, out_rows, D_MODEL), F32),
        compiler_params=_params("arbitrary", "arbitrary"),
        name="moe_combine_final" if final else "moe_combine",
    )(pos, h1.reshape(batch, lp, D_MODEL), route_t, g, y)


def _moe_layer(h1, route_t, g, wg, wu, wd, layer, final_g, batch, lp, s_real, final):
    tp = batch * lp
    n_blocks = -(-(tp * TOP_K + N_EXPERTS * (MOE_BLOCK - 1)) // MOE_BLOCK)
    n_blocks += n_blocks % 2
    pos, block_expert, pends = _dispatch_plan(route_t, n_blocks)
    buf = _moe_dispatch(h1, pos, pends, n_blocks * MOE_BLOCK)
    y = _moe_experts(buf, g, wg, wu, wd, layer, block_expert, n_blocks)
    return _moe_combine(h1, route_t, y, pos, final_g, batch, lp, s_real, final)


def _mla_rope(c, cos, sin_a, sin_b):
    half = B_ROPE // 2
    return c * cos + pltpu.roll(c, 128 - half, 1) * sin_a + pltpu.roll(c, half, 1) * sin_b


def _mla_proj_kernel(h_ref, g_ref, win_ref, qn_ref, kvn_ref, wq_ref, wk_ref, wv_ref,
                     qc_ref, qsa_ref, qsb_ref, kc_ref, ksa_ref, ksb_ref, q_ref, k_ref, v_ref):
    xn = _rms(h_ref[...], g_ref[...]).astype(BF16)
    c = jnp.dot(xn, win_ref[...], preferred_element_type=F32)
    cq = _rms(c[:, :B_Q_RANK], qn_ref[...]).astype(BF16)
    ckv = _rms(c[:, B_Q_RANK:B_Q_RANK + B_KV_RANK], kvn_ref[...]).astype(BF16)
    kpe = _mla_rope(c[:, B_Q_RANK + B_KV_RANK:], kc_ref[...], ksa_ref[...], ksb_ref[...])
    q = jnp.dot(cq, wq_ref[...], preferred_element_type=F32)
    k = jnp.dot(ckv, wk_ref[...], preferred_element_type=F32)
    v = jnp.dot(ckv, wv_ref[...], preferred_element_type=F32)
    qc, qsa, qsb = qc_ref[...], qsa_ref[...], qsb_ref[...]
    ones_col = (lax.broadcasted_iota(jnp.int32, kpe.shape, 1) == B_V).astype(F32)
    for hd in range(B_HEADS):
        sl = slice(hd * B_HEAD_PAD, (hd + 1) * B_HEAD_PAD)
        q_ref[0, hd] = _mla_rope(q[:, sl], qc, qsa, qsb).astype(BF16)
        k_ref[0, hd] = (k[:, sl] + kpe).astype(BF16)
        v_ref[0, hd] = (v[:, sl] + ones_col).astype(BF16)


def _mla_proj(h, g, win, qn, kvn, wq, wk, wv, tabs, batch, lp):
    tiles_per_batch = lp // ROW_TILE
    row = lambda b, i: (b * tiles_per_batch + i, 0)
    fixed = lambda b, i: (0, 0)
    tab = lambda b, i: (i, 0)
    head_out = pl.BlockSpec((1, B_HEADS, ROW_TILE, B_HEAD_PAD), lambda b, i: (b, 0, i, 0))
    hw = B_HEADS * B_HEAD_PAD
    shape = jax.ShapeDtypeStruct((batch, B_HEADS, lp, B_HEAD_PAD), BF16)
    return pl.pallas_call(
        _mla_proj_kernel,
        grid=(batch, tiles_per_batch),
        in_specs=[
            pl.BlockSpec((ROW_TILE, D_MODEL), row),
            pl.BlockSpec((1, D_MODEL), fixed),
            pl.BlockSpec((D_MODEL, 512), fixed),
            pl.BlockSpec((1, B_Q_RANK), fixed),
            pl.BlockSpec((1, B_KV_RANK), fixed),
            pl.BlockSpec((B_Q_RANK, hw), fixed),
            pl.BlockSpec((B_KV_RANK, hw), fixed),
            pl.BlockSpec((B_KV_RANK, hw), fixed),
        ] + [pl.BlockSpec((ROW_TILE, 128), tab)] * 6,
        out_specs=[head_out, head_out, head_out],
        out_shape=[shape, shape, shape],
        compiler_params=_params("parallel", "parallel"),
        name="mla_proj",
    )(h, g, win, qn, kvn, wq, wk, wv, *tabs)


def _mla_attn_kernel(q_ref, k_ref, v_ref, o_ref, m_sc, acc_sc, *, s_real, tq, tk):
    qi = pl.program_id(2)
    nq = s_real // tq
    is_meta_q = qi == nq
    nt = (((1,), (1,)), ((), ()))
    sub = min(MLA_SUB_ROWS, tq)
    n_sub = tq // sub

    m_sc[...] = jnp.full(m_sc.shape, NEG_INF, F32)
    acc_sc[...] = jnp.zeros(acc_sc.shape, F32)

    def item(r, key0, width, mask_fn, with_meta=False):
        rows = slice(r * sub, (r + 1) * sub)
        k = k_ref[0, 0, pl.ds(key0, width), :]
        if with_meta:
            k = jnp.concatenate([k_ref[0, 0, s_real:s_real + BLOCK, :], k], axis=0)
        s = lax.dot_general(q_ref[0, 0, rows, :], k, nt, preferred_element_type=F32)
        if mask_fn is not None:
            rr = lax.broadcasted_iota(jnp.int32, s.shape, 0) + r * sub
            cc = lax.broadcasted_iota(jnp.int32, s.shape, 1)
            s = jnp.where(mask_fn(rr, cc), s, NEG_INF)

        def finish():
            v = v_ref[0, 0, pl.ds(key0, width), :]
            if with_meta:
                v = jnp.concatenate([v_ref[0, 0, s_real:s_real + BLOCK, :], v], axis=0)
            m_old = m_sc[rows]
            m_new = jnp.maximum(m_old, jnp.max(s, axis=-1, keepdims=True))
            p = jnp.exp2(s - jnp.concatenate([m_new] * (s.shape[1] // 128), axis=1)).astype(BF16)
            acc_sc[rows] = acc_sc[rows] * jnp.exp2(m_old - m_new) + jnp.dot(p, v, preferred_element_type=F32)
            m_sc[rows] = m_new
        return finish

    def run(items):
        pending = None
        for make in items:
            nxt = make()
            if pending is not None:
                pending()
            pending = nxt
        pending()

    def full_body(j, carry):
        start = pl.multiple_of(j * tk, tk)
        run([functools.partial(item, r, start, tk, None) for r in range(n_sub)])
        return carry

    n_full = jnp.where(is_meta_q, 0, qi * (tq // tk))
    lax.fori_loop(0, n_full, full_body, 0)

    @pl.when(jnp.logical_not(is_meta_q))
    def _():
        base = pl.multiple_of(qi * tq, tq)
        diag_mask = lambda rr, cc: jnp.where(cc < BLOCK, cc, cc - BLOCK) <= jnp.where(cc < BLOCK, N_META - 1, rr)
        run([functools.partial(item, r, base, (r + 1) * sub, diag_mask, True) for r in range(n_sub)])
        acc = acc_sc[...]
        o_ref[0] = (acc * (1.0 / acc[:, B_V:B_V + 1])).astype(BF16)

    @pl.when(is_meta_q)
    def _():
        rows = min(sub, BLOCK)
        item(0, s_real, BLOCK, lambda rr, cc: (cc <= rr) & (cc < N_META))()
        acc = acc_sc[0:rows]
        o_ref[0, 0:rows] = (acc * (1.0 / acc[:, B_V:B_V + 1])).astype(BF16)


def _mla_attn(q, k, v, batch, lp, s_real):
    tq = min(MLA_Q_TILE, s_real)
    tk = min(MLA_K_CHUNK, tq)
    nq = s_real // tq
    qmap = lambda b, h, i: (b, h, i, 0)
    kvmap = lambda b, h, i: (b, h, 0, 0)
    return pl.pallas_call(
        functools.partial(_mla_attn_kernel, s_real=s_real, tq=tq, tk=tk),
        grid=(batch, B_HEADS, nq + 1),
        in_specs=[
            pl.BlockSpec((1, 1, tq, B_HEAD_PAD), qmap),
            pl.BlockSpec((1, 1, lp, B_HEAD_PAD), kvmap),
            pl.BlockSpec((1, 1, lp, B_HEAD_PAD), kvmap),
        ],
        out_specs=pl.BlockSpec((1, tq, B_HEAD_PAD), lambda b, h, i: (b, i, h)),
        out_shape=jax.ShapeDtypeStruct((batch, lp, B_HEADS * B_HEAD_PAD), BF16),
        scratch_shapes=[pltpu.VMEM((tq, 128), F32), pltpu.VMEM((tq, B_HEAD_PAD), F32)],
        compiler_params=_params("parallel", "parallel", "arbitrary"),
        name="mla_attn",
    )(q, k, v)


def _positions(lp, s_real):
    r = jnp.arange(lp, dtype=jnp.int32)
    return jnp.where(r < s_real, r + N_META, r - s_real).astype(F32)


def _rope_angles(pos, dim):
    inv_freq = 1.0 / (ROPE_THETA ** (jnp.arange(0, dim, 2, dtype=F32) / dim))
    ang = pos[:, None] * inv_freq[None, :]
    return jnp.cos(ang), jnp.sin(ang)


def _swa_tables(pos):
    cos, sin = _rope_angles(pos, A_HEAD_DIM)
    cos_t = jnp.tile(cos, (1, 4))
    sin_t = jnp.tile(jnp.concatenate([-sin, sin], axis=1), (1, 2))
    return cos_t, sin_t


def _mla_tables(pos, scale):
    cos, sin = _rope_angles(pos, B_ROPE)
    n = pos.shape[0]
    z16 = jnp.zeros((n, B_ROPE // 2), F32)
    tail = jnp.zeros((n, 128 - B_NOPE - B_ROPE), F32)
    ones = jnp.ones((n, B_NOPE), F32)
    zeros = jnp.zeros((n, B_NOPE), F32)
    cos_t = jnp.concatenate([ones, cos, cos, tail], axis=1) * scale
    sin_a = jnp.concatenate([zeros, -sin, z16, tail], axis=1) * scale
    sin_b = jnp.concatenate([zeros, z16, sin, tail], axis=1) * scale
    return cos_t, sin_a, sin_b


def _pad_heads(w, n_heads, width, pad_to):
    k = w.shape[0]
    w = w.reshape(k, n_heads, width)
    w = jnp.pad(w, ((0, 0), (0, 0), (0, pad_to - width)))
    return w.reshape(k, n_heads * pad_to)


def kernel(x, meta_tokens, attn_norm, ffn_norm, final_norm, a_w_qkv, a_b_qkv, a_sinks, a_w_o, a_b_o,
           b_w_in, b_q_norm, b_kv_norm, b_w_uq, b_w_ukv, b_w_o,
           moe_w_group, moe_b_group, moe_w_router, moe_b_router, moe_w_gate, moe_w_up, moe_w_down):
    batch, s_real, _ = x.shape
    lp = s_real + BLOCK
    tp = batch * lp
    assert tp % ROW_TILE == 0 and lp % ROW_TILE == 0 and s_real % BLOCK == 0

    meta = jnp.broadcast_to(meta_tokens[None].astype(x.dtype), (batch, N_META, D_MODEL))
    pad = jnp.zeros((batch, BLOCK - N_META, D_MODEL), x.dtype)
    h = jnp.concatenate([x, meta, pad], axis=1).reshape(tp, D_MODEL)
    pos = _positions(lp, s_real)

    def router_params(i):
        wr = jnp.concatenate([moe_w_router[i], moe_w_group[i]], axis=1)
        wr = jnp.pad(wr, ((0, 0), (0, ROUTE_LANES - wr.shape[1])))
        br = jnp.concatenate([moe_b_router[i], moe_b_group[i]])
        br = jnp.pad(br, (0, ROUTE_LANES - br.shape[0]))[None]
        w_hi = wr.astype(BF16)
        w_lo = (wr - w_hi.astype(F32)).astype(BF16)
        return jnp.concatenate([w_hi, w_lo], axis=1), br

    w_gate = moe_w_gate.reshape(-1, D_MODEL, D_EXPERT)
    w_up = moe_w_up.reshape(-1, D_MODEL, D_EXPERT)
    w_down = moe_w_down.reshape(-1, D_EXPERT, D_MODEL)

    def moe(i, h1, route_t, final):
        return _moe_layer(h1, route_t, ffn_norm[i][None], w_gate, w_up, w_down, i,
                          final_norm[None], batch, lp, s_real, final)

    wq, wk, wv = a_w_qkv[0][:, :A_Q_W], a_w_qkv[0][:, A_Q_W:A_Q_W + A_KV_W], a_w_qkv[0][:, A_Q_W + A_KV_W:]
    bq, bk, bv = a_b_qkv[0][:A_Q_W], a_b_qkv[0][A_Q_W:A_Q_W + A_KV_W], a_b_qkv[0][A_Q_W + A_KV_W:]
    swap = lambda t: jnp.concatenate([t[..., A_HEAD_DIM:], t[..., :A_HEAD_DIM]], axis=-1)
    w_a = jnp.concatenate([wq, wk, swap(wk), wv, swap(wv)], axis=1).astype(BF16)
    b_a = jnp.concatenate([bq, bk, swap(bk), bv, swap(bv)])[None]
    cos_a, sin_a = _swa_tables(pos)
    q, kk, vv = _swa_qkv(h, attn_norm[0][None], w_a, b_a, cos_a, sin_a, lp)
    sink_rows = jnp.repeat(a_sinks[0].astype(F32).reshape(A_KV_HEADS, 4, 2).transpose(0, 2, 1).reshape(4, 4), BLOCK, axis=1)[:, None, :]
    o = _swa_attn(q, kk, vv, sink_rows, batch, lp, s_real)
    wr, br = router_params(0)
    h1, route_t = _attn_out(o, a_w_o[0].astype(BF16), a_b_o[0][None], h, ffn_norm[0][None], wr, br)
    h = moe(0, h1, route_t, False).reshape(tp, D_MODEL)

    scale = (B_NOPE + B_ROPE) ** -0.5
    w_in = b_w_in[0]
    kpe_cols = jnp.pad(w_in[:, B_Q_RANK + B_KV_RANK:], ((0, 0), (B_NOPE, 128 - B_NOPE - B_ROPE)))
    w_in_p = jnp.concatenate([w_in[:, :B_Q_RANK + B_KV_RANK], kpe_cols], axis=1).astype(BF16)
    w_q_p = _pad_heads(b_w_uq[0], B_HEADS, B_NOPE + B_ROPE, B_HEAD_PAD).astype(BF16)
    w_ukv = b_w_ukv[0].reshape(B_KV_RANK, B_HEADS, B_NOPE + B_V)
    w_k_p = _pad_heads(w_ukv[:, :, :B_NOPE].reshape(B_KV_RANK, -1), B_HEADS, B_NOPE, B_HEAD_PAD).astype(BF16)
    w_v_p = _pad_heads(w_ukv[:, :, B_NOPE:].reshape(B_KV_RANK, -1), B_HEADS, B_V, B_HEAD_PAD).astype(BF16)
    tabs = _mla_tables(pos, scale * math.log2(math.e)) + _mla_tables(pos, 1.0)
    q, k, v = _mla_proj(h, attn_norm[1][None], w_in_p, b_q_norm[0][None], b_kv_norm[0][None],
                        w_q_p, w_k_p, w_v_p, tabs, batch, lp)
    o = _mla_attn(q, k, v, batch, lp, s_real).reshape(tp, B_HEADS * B_HEAD_PAD)
    w_o_p = jnp.pad(b_w_o[0].reshape(B_HEADS, B_V, D_MODEL), ((0, 0), (0, B_HEAD_PAD - B_V), (0, 0)))
    w_o_p = w_o_p.reshape(B_HEADS * B_HEAD_PAD, D_MODEL).astype(BF16)
    wr, br = router_params(1)
    h1, route_t = _attn_out(o, w_o_p, jnp.zeros((1, D_MODEL), F32), h, ffn_norm[1][None], wr, br)
    return moe(1, h1, route_t, True)
```

```python
import functools
import math

import jax
import jax.numpy as jnp
from jax import lax
from jax.experimental import pallas as pl
from jax.experimental.pallas import tpu as pltpu

F32 = jnp.float32
BF16 = jnp.bfloat16

D_MODEL = 1024
N_META = 16
BLOCK = 128
ROPE_THETA = 10000.0
NORM_EPS = 1e-6
NEG_INF = -1e30

A_HEADS = 16
A_KV_HEADS = 2
A_HEAD_DIM = 64
A_Q_W = A_HEADS * A_HEAD_DIM
A_KV_W = A_KV_HEADS * A_HEAD_DIM

B_HEADS = 16
B_NOPE = 64
B_ROPE = 32
B_V = 64
B_Q_RANK = 256
B_KV_RANK = 128
B_HEAD_PAD = 128

N_GROUPS = 4
EXPERTS_PER_GROUP = 8
N_EXPERTS = N_GROUPS * EXPERTS_PER_GROUP
TOP_K = 2
D_EXPERT = 256
MOE_BLOCK = 128
ROUTE_LANES = 128

ROW_TILE = 640
DMA_UNROLL = 8
MLA_Q_TILE = 2048
MLA_K_CHUNK = 1024
MLA_SUB_ROWS = 256
VMEM_LIMIT = 56 * 1024 * 1024


def _rms(x, g):
    return x * lax.rsqrt(jnp.mean(x * x, axis=-1, keepdims=True) + NORM_EPS) * g


def _params(*sem):
    return pltpu.CompilerParams(dimension_semantics=sem, vmem_limit_bytes=VMEM_LIMIT)


def _swa_qkv_kernel(h_ref, g_ref, w_ref, b_ref, cos_ref, sin_ref, q_ref, k_ref, v_ref):
    xn = _rms(h_ref[...], g_ref[...]).astype(BF16)
    y = jnp.dot(xn, w_ref[...], preferred_element_type=F32) + b_ref[...]
    cos = cos_ref[...]
    sin = sin_ref[...]
    lane = lax.broadcasted_iota(jnp.int32, cos.shape, 1)
    first = (lane & (A_HEAD_DIM - 1)) < (A_HEAD_DIM // 2)

    def rope(c):
        rot = jnp.where(first, pltpu.roll(c, 128 - A_HEAD_DIM // 2, 1), pltpu.roll(c, A_HEAD_DIM // 2, 1))
        return c * cos + rot * sin

    for j in range(A_Q_W // 128):
        q_ref[:, j * 128:(j + 1) * 128] = (rope(y[:, j * 128:(j + 1) * 128]) * (A_HEAD_DIM ** -0.5)).astype(BF16)
    k_ref[:, 0:128] = rope(y[:, A_Q_W:A_Q_W + 128]).astype(BF16)
    k_ref[:, 128:256] = rope(y[:, A_Q_W + 128:A_Q_W + 256]).astype(BF16)
    v_ref[...] = y[:, A_Q_W + 256:A_Q_W + 512].astype(BF16)


def _swa_qkv(h, g, w, b, cos, sin, lp):
    tp = h.shape[0]
    tiles_per_batch = lp // ROW_TILE
    n_out = w.shape[1]
    row = lambda i: (i, 0)
    fixed = lambda i: (0, 0)
    tab = lambda i: (i % tiles_per_batch, 0)
    return pl.pallas_call(
        _swa_qkv_kernel,
        grid=(tp // ROW_TILE,),
        in_specs=[
            pl.BlockSpec((ROW_TILE, D_MODEL), row),
            pl.BlockSpec((1, D_MODEL), fixed),
            pl.BlockSpec((D_MODEL, n_out), fixed),
            pl.BlockSpec((1, n_out), fixed),
            pl.BlockSpec((ROW_TILE, 128), tab),
            pl.BlockSpec((ROW_TILE, 128), tab),
        ],
        out_specs=[
            pl.BlockSpec((ROW_TILE, A_Q_W), row),
            pl.BlockSpec((ROW_TILE, 256), row),
            pl.BlockSpec((ROW_TILE, 256), row),
        ],
        out_shape=[
            jax.ShapeDtypeStruct((tp, A_Q_W), BF16),
            jax.ShapeDtypeStruct((tp, 256), BF16),
            jax.ShapeDtypeStruct((tp, 256), BF16),
        ],
        compiler_params=_params("parallel"),
        name="swa_qkv",
    )(h, g, w, b, cos, sin)


def _swa_attn_kernel(q_ref, kc_ref, kp_ref, km_ref, vc_ref, vp_ref, vm_ref, sink_ref, o_ref, *, n_real_blocks):
    n = pl.program_id(1)
    is_real = n < n_real_blocks
    n_keys = N_META + 2 * BLOCK
    big = jnp.int32(1 << 20)
    key = lax.broadcasted_iota(jnp.int32, (n_keys, BLOCK), 0)
    qry = lax.broadcasted_iota(jnp.int32, (n_keys, BLOCK), 1)
    jp = key - N_META
    jc = key - (N_META + BLOCK)
    meta_lim = jnp.where(is_real, N_META, 0)
    prev_off = jnp.where(jnp.logical_and(is_real, n >= 1), 0, big)
    cur_lim = jnp.where(is_real, big, N_META)
    ok = (key < meta_lim) | ((jc < 0) & (jp > qry + prev_off)) | ((jc >= 0) & (jc <= qry) & (jc < cur_lim))
    bias1 = jnp.where(ok, 0.0, NEG_INF).astype(F32)
    bias = jnp.concatenate([bias1] * 4, axis=1)

    kcat = jnp.concatenate([km_ref[0], kp_ref[0], kc_ref[0]], axis=0)
    vcat = jnp.concatenate([vm_ref[0], vp_ref[0], vc_ref[0]], axis=0)
    lo = lax.broadcasted_iota(jnp.int32, (n_keys, 128), 1) < A_HEAD_DIM
    zero = jnp.zeros((n_keys, 128), BF16)

    def scores(kvh, parity):
        a, b = (kcat[:, :128], kcat[:, 128:]) if kvh == 0 else (kcat[:, 128:], kcat[:, :128])
        kx = jnp.where(lo, a, zero) if parity == 0 else jnp.where(lo, zero, b)
        base = kvh * (A_Q_W // A_KV_HEADS)
        qs = jnp.concatenate([q_ref[0, :, base + c * 128:base + (c + 1) * 128] for c in range(4)], axis=0)
        st = lax.dot_general(kx, qs, (((1,), (1,)), ((), ())), preferred_element_type=F32) + bias

        def finish():
            a, b = (vcat[:, :128], vcat[:, 128:]) if kvh == 0 else (vcat[:, 128:], vcat[:, :128])
            vx = jnp.where(lo, a, zero) if parity == 0 else jnp.where(lo, zero, b)
            sink = sink_ref[kvh * 2 + parity]
            m = jnp.maximum(jnp.max(st, axis=0, keepdims=True), sink)
            p = jnp.exp(st - m)
            denom = jnp.sum(p, axis=0, keepdims=True) + jnp.exp(sink - m)
            ot = lax.dot_general(vx, p.astype(BF16), (((0,), (0,)), ((), ())), preferred_element_type=F32)
            return ot * (1.0 / denom)
        return finish

    for kvh in range(A_KV_HEADS):
        base = kvh * (A_Q_W // A_KV_HEADS)
        even = scores(kvh, 0)
        odd = scores(kvh, 1)
        ot = even() + odd()
        for c in range(4):
            o_ref[0, :, base + c * 128:base + (c + 1) * 128] = ot[:, c * BLOCK:(c + 1) * BLOCK].T.astype(BF16)


def _swa_attn(q, kk, vv, sink_rows, batch, lp, s_real):
    nb = s_real // BLOCK
    q3 = q.reshape(batch, lp, A_Q_W)
    k3 = kk.reshape(batch, lp, 256)
    v3 = vv.reshape(batch, lp, 256)
    cur = lambda b, n: (b, n, 0)
    prev = lambda b, n: (b, jnp.maximum(n - 1, 0), 0)
    meta = lambda b, n: (b, s_real // N_META, 0)
    out = pl.pallas_call(
        functools.partial(_swa_attn_kernel, n_real_blocks=nb),
        grid=(batch, nb + 1),
        in_specs=[
            pl.BlockSpec((1, BLOCK, A_Q_W), cur),
            pl.BlockSpec((1, BLOCK, 256), cur),
            pl.BlockSpec((1, BLOCK, 256), prev),
            pl.BlockSpec((1, N_META, 256), meta),
            pl.BlockSpec((1, BLOCK, 256), cur),
            pl.BlockSpec((1, BLOCK, 256), prev),
            pl.BlockSpec((1, N_META, 256), meta),
            pl.BlockSpec((4, 1, 4 * BLOCK), lambda b, n: (0, 0, 0)),
        ],
        out_specs=pl.BlockSpec((1, BLOCK, A_Q_W), cur),
        out_shape=jax.ShapeDtypeStruct((batch, lp, A_Q_W), BF16),
        compiler_params=_params("parallel", "parallel"),
        name="swa_attn",
    )(q3, k3, k3, k3, v3, v3, v3, sink_rows)
    return out.reshape(batch * lp, A_Q_W)


ROUTE_ROWS = 40


def _route(xn, wr_ref, br):
    x_hi = xn.astype(BF16)
    x_lo = (xn - x_hi.astype(F32)).astype(BF16)
    both = jnp.dot(x_hi, wr_ref[...], preferred_element_type=F32)
    lg = (both[:, :ROUTE_LANES] + both[:, ROUTE_LANES:]
          + jnp.dot(x_lo, wr_ref[:, :ROUTE_LANES], preferred_element_type=F32) + br)
    lt = lg.T[0:ROUTE_ROWS]
    row_i = lax.broadcasted_iota(jnp.int32, lt.shape, 0)
    row = row_i.astype(F32)
    row_grp = (row_i >> 3).astype(F32)
    big = 1e9
    is_g = jnp.logical_and(row_i >= N_EXPERTS, row_i < N_EXPERTS + N_GROUPS)
    gl = jnp.where(is_g, lt, -jnp.inf)
    gmax = jnp.max(gl, axis=0, keepdims=True)
    g_p = 1.0 / jnp.sum(jnp.exp(gl - gmax), axis=0, keepdims=True)
    g_idx = jnp.min(jnp.where(gl == gmax, row - float(N_EXPERTS), big), axis=0, keepdims=True)
    el = jnp.where(row_grp == g_idx, lt, -jnp.inf)
    m1 = jnp.max(el, axis=0, keepdims=True)
    esum = jnp.sum(jnp.exp(el - m1), axis=0, keepdims=True)
    i1 = jnp.min(jnp.where(el == m1, row, big), axis=0, keepdims=True)
    el2 = jnp.where(row == i1, -jnp.inf, el)
    m2 = jnp.max(el2, axis=0, keepdims=True)
    i2 = jnp.min(jnp.where(el2 == m2, row, big), axis=0, keepdims=True)
    e1 = 1.0 / esum
    e2 = jnp.exp(m2 - m1) / esum
    w1 = g_p * e1 / (e1 + e2)
    w2 = g_p * e2 / (e1 + e2)
    return jnp.concatenate([i1, i2, w1, w2, jnp.zeros((4, lt.shape[1]), F32)], axis=0)


def _attn_out_kernel(o_ref, w_ref, b_ref, h_ref, g_ref, wr_ref, br_ref, h1_ref, route_t_ref):
    h1 = h_ref[...] + jnp.dot(o_ref[...], w_ref[...], preferred_element_type=F32) + b_ref[...]
    h1_ref[...] = h1
    route_t_ref[...] = _route(_rms(h1, g_ref[...]), wr_ref, br_ref[...])


def _attn_out(o, w, b, h, g, wr, br):
    tp, ko = o.shape
    row = lambda i: (i, 0)
    fixed = lambda i: (0, 0)
    return pl.pallas_call(
        _attn_out_kernel,
        grid=(tp // ROW_TILE,),
        in_specs=[
            pl.BlockSpec((ROW_TILE, ko), row),
            pl.BlockSpec((ko, D_MODEL), fixed),
            pl.BlockSpec((1, D_MODEL), fixed),
            pl.BlockSpec((ROW_TILE, D_MODEL), row),
            pl.BlockSpec((1, D_MODEL), fixed),
            pl.BlockSpec((D_MODEL, 2 * ROUTE_LANES), fixed),
            pl.BlockSpec((1, ROUTE_LANES), fixed),
        ],
        out_specs=[
            pl.BlockSpec((ROW_TILE, D_MODEL), row),
            pl.BlockSpec((8, ROW_TILE), lambda i: (0, i)),
        ],
        out_shape=[
            jax.ShapeDtypeStruct((tp, D_MODEL), F32),
            jax.ShapeDtypeStruct((8, tp), F32),
        ],
        compiler_params=_params("parallel"),
        name="attn_out_route",
    )(o, w, b, h, g, wr, br)


def _plan_kernel(rt_ref, pos_ref, be_ref, pends_ref, cnt_col, carry, *, n_blocks_pad):
    phase = pl.program_id(0)
    i = pl.program_id(1)
    tm = rt_ref.shape[1]
    id0 = rt_ref[0:1, :]
    id1 = rt_ref[1:2, :]
    e_sub = lax.broadcasted_iota(jnp.int32, (N_EXPERTS, tm), 0).astype(F32)
    hit0 = e_sub == id0
    hit1 = e_sub == id1
    member_t = jnp.where(hit0, 1.0, jnp.where(hit1, 1.0, 0.0))
    tile_cnt_col = jnp.sum(member_t, axis=1, keepdims=True)

    @pl.when(jnp.logical_and(phase == 0, i == 0))
    def _():
        cnt_col[...] = jnp.zeros(cnt_col.shape, F32)

    @pl.when(phase == 0)
    def _():
        cnt_col[...] += jnp.broadcast_to(tile_cnt_col, cnt_col.shape)

    @pl.when(jnp.logical_and(phase == 1, i == 0))
    def _():
        pad_to_block = lambda c: jnp.floor((c + (MOE_BLOCK - 1.0)) * (1.0 / MOE_BLOCK)) * MOE_BLOCK
        padded_col = pad_to_block(cnt_col[...])
        square = jnp.concatenate([padded_col, jnp.zeros((128 - N_EXPERTS, 128), F32)], axis=0)
        padded_row = square.T[0:1]
        lane_e = lax.broadcasted_iota(jnp.int32, (N_EXPERTS, 128), 1)
        sub_e = lax.broadcasted_iota(jnp.int32, (N_EXPERTS, 128), 0)
        pstart_col = jnp.sum(jnp.where(lane_e < sub_e, padded_row, 0.0), axis=1, keepdims=True)
        pends_row = jnp.sum(jnp.where(sub_e <= lane_e, padded_col, 0.0), axis=0, keepdims=True)
        pends_col = pstart_col + padded_col[:, 0:1]
        carry[...] = jnp.broadcast_to(pstart_col, carry.shape)
        pends_ref[...] = pends_row.astype(jnp.int32)
        blk_lane = lax.broadcasted_iota(jnp.int32, (N_EXPERTS, n_blocks_pad), 1)
        blk_start = (blk_lane * MOE_BLOCK).astype(F32)
        be = jnp.sum(jnp.where(pends_col <= blk_start, 1.0, 0.0), axis=0, keepdims=True)
        be = jnp.minimum(be, N_EXPERTS - 1.0)
        n_used = pends_row[:, N_EXPERTS - 1:N_EXPERTS] * (1.0 / MOE_BLOCK)
        be = jnp.where(blk_lane[0:1] == n_blocks_pad - 1, n_used, be)
        be_ref[...] = be.astype(jnp.int32)

    @pl.when(phase == 1)
    def _():
        before = lax.broadcasted_iota(jnp.int32, (tm, tm), 0) < lax.broadcasted_iota(jnp.int32, (tm, tm), 1)
        prefix = jnp.dot(member_t.astype(BF16), jnp.where(before, 1.0, 0.0).astype(BF16),
                         preferred_element_type=F32)
        row_of = prefix + carry[:, 0:1]
        dest0 = jnp.sum(jnp.where(hit0, row_of, 0.0), axis=0, keepdims=True)
        dest1 = jnp.sum(jnp.where(hit1, row_of, 0.0), axis=0, keepdims=True)
        pos_ref[...] = jnp.concatenate([dest0, dest1], axis=0).astype(jnp.int32)
        carry[...] += jnp.broadcast_to(tile_cnt_col, carry.shape)


def _dispatch_plan(route_t, n_blocks):
    tp = route_t.shape[1]
    n_tiles = tp // ROW_TILE
    n_blocks_pad = -(-(n_blocks + 1) // 128) * 128
    pos, be, pends = pl.pallas_call(
        functools.partial(_plan_kernel, n_blocks_pad=n_blocks_pad),
        grid=(2, n_tiles),
        in_specs=[
            pl.BlockSpec((8, ROW_TILE), lambda p, i: (0, i)),
        ],
        out_specs=[
            pl.BlockSpec((TOP_K, ROW_TILE), lambda p, i: (0, i * p)),
            pl.BlockSpec((1, n_blocks_pad), lambda p, i: (0, 0)),
            pl.BlockSpec((1, 128), lambda p, i: (0, 0)),
        ],
        out_shape=[
            jax.ShapeDtypeStruct((TOP_K, tp), jnp.int32),
            jax.ShapeDtypeStruct((1, n_blocks_pad), jnp.int32),
            jax.ShapeDtypeStruct((1, 128), jnp.int32),
        ],
        scratch_shapes=[pltpu.VMEM((N_EXPERTS, 128), F32), pltpu.VMEM((N_EXPERTS, 128), F32)],
        compiler_params=_params("arbitrary", "arbitrary"),
        name="moe_plan",
    )(route_t)
    return pos.reshape(-1), be.reshape(-1), pends.reshape(-1)


def _scatter_row(stage, slot, r, buf_hbm, dst_row, sem):
    return pltpu.make_async_copy(stage.at[slot, pl.ds(r, 1)], buf_hbm.at[pl.ds(dst_row, 1)], sem.at[slot])


def _dispatch_kernel(pos_ref, pends_ref, h_ref, buf_hbm, stage, zsem, sem, *, n_tokens, n_blocks):
    i = pl.program_id(0)
    slot = i % 2

    def zero_block(row0):
        return pltpu.make_async_copy(stage.at[0], buf_hbm.at[pl.ds(pl.multiple_of(row0, MOE_BLOCK), MOE_BLOCK)], zsem)

    def zero_tail(e):
        return zero_block(jnp.maximum(pends_ref[e] - MOE_BLOCK, 0))

    @pl.when(i == 0)
    def _():
        stage[0] = jnp.zeros((BLOCK, D_MODEL), F32)
        for e in range(N_EXPERTS):
            zero_tail(e).start()
        for e in range(N_EXPERTS):
            zero_tail(e).wait()
        first_unused = pends_ref[N_EXPERTS - 1] // MOE_BLOCK

        def start_unused(b, carry):
            zero_block(b * MOE_BLOCK).start()
            return carry

        def wait_unused(b, carry):
            zero_block(b * MOE_BLOCK).wait()
            return carry
        lax.fori_loop(first_unused, n_blocks, start_unused, 0)
        lax.fori_loop(first_unused, n_blocks, wait_unused, 0)

    stage[slot] = h_ref[...]

    def issue(c, carry):
        for u in range(DMA_UNROLL):
            r = pl.multiple_of(c * DMA_UNROLL, DMA_UNROLL) + u
            t = i * BLOCK + r
            _scatter_row(stage, slot, r, buf_hbm, pos_ref[t], sem).start()
            _scatter_row(stage, slot, r, buf_hbm, pos_ref[n_tokens + t], sem).start()
        return carry
    lax.fori_loop(0, BLOCK // DMA_UNROLL, issue, 0)

    def drain(which):
        for _ in range(2 * BLOCK):
            _scatter_row(stage, which, 0, buf_hbm, 0, sem).wait()

    @pl.when(i > 0)
    def _():
        drain(1 - slot)

    @pl.when(i == pl.num_programs(0) - 1)
    def _():
        drain(slot)


def _moe_dispatch(h1, pos, pends, n_rows):
    tp = h1.shape[0]
    grid_spec = pltpu.PrefetchScalarGridSpec(
        num_scalar_prefetch=2,
        grid=(tp // BLOCK,),
        in_specs=[pl.BlockSpec((BLOCK, D_MODEL), lambda i, pos, pends: (i, 0))],
        out_specs=pl.BlockSpec(memory_space=pl.ANY),
        scratch_shapes=[
            pltpu.VMEM((2, BLOCK, D_MODEL), F32),
            pltpu.SemaphoreType.DMA(()),
            pltpu.SemaphoreType.DMA((2,)),
        ],
    )
    return pl.pallas_call(
        functools.partial(_dispatch_kernel, n_tokens=tp, n_blocks=n_rows // MOE_BLOCK),
        grid_spec=grid_spec,
        out_shape=jax.ShapeDtypeStruct((n_rows, D_MODEL), F32),
        compiler_params=_params("arbitrary"),
        name="moe_dispatch",
    )(pos, pends, h1)


def _moe_kernel(be_ref, x_ref, g_ref, wga_ref, wua_ref, wda_ref, wgb_ref, wub_ref, wdb_ref, y_ref,
                wg_bf, wu_bf, wd_bf, *, n_blocks_pad):
    i = pl.program_id(0)
    n_used = be_ref[n_blocks_pad - 1]
    weights = ((wga_ref, wua_ref, wda_ref), (wgb_ref, wub_ref, wdb_ref))

    def expert_of(blk):
        return be_ref[jnp.clip(blk, 0, n_used - 1)]

    for half, (wg_ref, wu_ref, wd_ref) in enumerate(weights):
        blk = 2 * i + half

        @pl.when(jnp.logical_or(i == 0, expert_of(blk) != expert_of(blk - 2)))
        def _():
            wg_bf[half] = wg_ref[0].astype(BF16)
            wu_bf[half] = wu_ref[0].astype(BF16)
            wd_bf[half] = wd_ref[0].astype(BF16)

    def rows(half):
        return slice(half * MOE_BLOCK, (half + 1) * MOE_BLOCK)

    def gate_up(half):
        xn = _rms(x_ref[rows(half), :], g_ref[...]).astype(BF16)
        return (jnp.dot(xn, wg_bf[half], preferred_element_type=F32),
                jnp.dot(xn, wu_bf[half], preferred_element_type=F32))

    def down(half, gate, up):
        act = (gate * jax.nn.sigmoid(gate) * up).astype(BF16)
        y_ref[rows(half), :] = jnp.dot(act, wd_bf[half], preferred_element_type=F32)

    def zero(half):
        y_ref[rows(half), :] = jnp.zeros((MOE_BLOCK, D_MODEL), F32)

    used_a = 2 * i < n_used
    used_b = 2 * i + 1 < n_used

    @pl.when(used_b)
    def _():
        ga, ua = gate_up(0)
        gb, ub = gate_up(1)
        down(0, ga, ua)
        down(1, gb, ub)

    @pl.when(jnp.logical_and(used_a, jnp.logical_not(used_b)))
    def _():
        down(0, *gate_up(0))
        zero(1)

    @pl.when(jnp.logical_not(used_a))
    def _():
        zero(0)
        zero(1)


def _moe_experts(buf, g, wg, wu, wd, layer, block_expert, n_blocks):
    n_blocks_pad = block_expert.shape[0]
    assert n_blocks % 2 == 0
    last = lambda be: be[n_blocks_pad - 1] - 1
    pair = lambda i, be: (jnp.minimum(i, last(be) // 2), 0)
    ew = lambda half: (lambda i, be: (layer * N_EXPERTS + be[jnp.minimum(2 * i + half, last(be))], 0, 0))
    weights = lambda half: [
        pl.BlockSpec((1, D_MODEL, D_EXPERT), ew(half)),
        pl.BlockSpec((1, D_MODEL, D_EXPERT), ew(half)),
        pl.BlockSpec((1, D_EXPERT, D_MODEL), ew(half)),
    ]
    grid_spec = pltpu.PrefetchScalarGridSpec(
        num_scalar_prefetch=1,
        grid=(n_blocks // 2,),
        in_specs=[
            pl.BlockSpec((2 * MOE_BLOCK, D_MODEL), pair),
            pl.BlockSpec((1, D_MODEL), lambda i, be: (0, 0)),
        ] + weights(0) + weights(1),
        out_specs=pl.BlockSpec((2 * MOE_BLOCK, D_MODEL), lambda i, be: (i, 0)),
        scratch_shapes=[
            pltpu.VMEM((2, D_MODEL, D_EXPERT), BF16),
            pltpu.VMEM((2, D_MODEL, D_EXPERT), BF16),
            pltpu.VMEM((2, D_EXPERT, D_MODEL), BF16),
        ],
    )
    return pl.pallas_call(
        functools.partial(_moe_kernel, n_blocks_pad=n_blocks_pad),
        grid_spec=grid_spec,
        out_shape=jax.ShapeDtypeStruct((n_blocks * MOE_BLOCK, D_MODEL), F32),
        compiler_params=_params("arbitrary"),
        name="moe_experts",
    )(block_expert, buf, g, wg, wu, wd, wg, wu, wd)


def _row_copy(src_hbm, src_row, dst, slot, dst_row, sem):
    return pltpu.make_async_copy(src_hbm.at[pl.ds(src_row, 1)], dst.at[slot, pl.ds(dst_row, 1)], sem.at[slot])


def _combine_kernel(pos_ref, h_ref, rt_ref, g_ref, y_hbm, o_ref, buf, sem, *, tiles_per_batch, n_tokens, final):
    b = pl.program_id(0)
    j = pl.program_id(1)
    nj = pl.num_programs(1)
    step = b * nj + j
    slot = step % 2

    def issue(bb, jj, to_slot):
        tile = bb * tiles_per_batch + jj
        def body(c, carry):
            for u in range(DMA_UNROLL):
                r = pl.multiple_of(c * DMA_UNROLL, DMA_UNROLL) + u
                t = tile * BLOCK + r
                _row_copy(y_hbm, pos_ref[t], buf, to_slot, r, sem).start()
                _row_copy(y_hbm, pos_ref[n_tokens + t], buf, to_slot, BLOCK + r, sem).start()
            return carry
        lax.fori_loop(0, BLOCK // DMA_UNROLL, body, 0)

    @pl.when(step == 0)
    def _():
        issue(b, j, 0)

    @pl.when(step + 1 < pl.num_programs(0) * nj)
    def _():
        nxt = j + 1
        wrap = nxt == nj
        issue(jnp.where(wrap, b + 1, b), jnp.where(wrap, 0, nxt), 1 - slot)

    for _ in range(2 * BLOCK):
        _row_copy(y_hbm, 0, buf, slot, 0, sem).wait()

    w = jnp.concatenate([rt_ref[...], jnp.zeros((BLOCK - 8, BLOCK), F32)], axis=0).T
    h2 = h_ref[0] + w[:, TOP_K:TOP_K + 1] * buf[slot, 0:BLOCK] + w[:, TOP_K + 1:TOP_K + 2] * buf[slot, BLOCK:2 * BLOCK]
    if final:
        h2 = _rms(h2, g_ref[...])
    o_ref[0] = h2


def _moe_combine(h1, route_t, y, pos, g, batch, lp, s_real, final):
    tiles_per_batch = lp // BLOCK
    nj = s_real // BLOCK if final else tiles_per_batch
    out_rows = s_real if final else lp
    tile = lambda b, j, pos: (b, j, 0)
    grid_spec = pltpu.PrefetchScalarGridSpec(
        num_scalar_prefetch=1,
        grid=(batch, nj),
        in_specs=[
            pl.BlockSpec((1, BLOCK, D_MODEL), tile),
            pl.BlockSpec((8, BLOCK), lambda b, j, pos: (0, b * tiles_per_batch + j)),
            pl.BlockSpec((1, D_MODEL), lambda b, j, pos: (0, 0)),
            pl.BlockSpec(memory_space=pl.ANY),
        ],
        out_specs=pl.BlockSpec((1, BLOCK, D_MODEL), tile),
        scratch_shapes=[
            pltpu.VMEM((2, 2 * BLOCK, D_MODEL), F32),
            pltpu.SemaphoreType.DMA((2,)),
        ],
    )
    return pl.pallas_call(
        functools.partial(_combine_kernel, tiles_per_batch=tiles_per_batch, n_tokens=batch * lp, final=final),
        grid_spec=grid_spec,
        out_shape=jax.ShapeDtypeStruct((batch, out_rows, D_MODEL), F32),
        compiler_params=_params("arbitrary", "arbitrary"),
        name="moe_combine_final" if final else "moe_combine",
    )(pos, h1.reshape(batch, lp, D_MODEL), route_t, g, y)


def _moe_layer(h1, route_t, g, wg, wu, wd, layer, final_g, batch, lp, s_real, final):
    tp = batch * lp
    n_blocks = -(-(tp * TOP_K + N_EXPERTS * (MOE_BLOCK - 1)) // MOE_BLOCK)
    n_blocks += n_blocks % 2
    pos, block_expert, pends = _dispatch_plan(route_t, n_blocks)
    buf = _moe_dispatch(h1, pos, pends, n_blocks * MOE_BLOCK)
    y = _moe_experts(buf, g, wg, wu, wd, layer, block_expert, n_blocks)
    return _moe_combine(h1, route_t, y, pos, final_g, batch, lp, s_real, final)


def _mla_rope(c, cos, sin):
    return c * cos + pltpu.roll(c, 128 - B_ROPE // 2, 1) * sin


def _mla_proj_kernel(h_ref, g_ref, win_ref, qn_ref, kvn_ref, wq_ref, wk_ref, wv_ref,
                     qc_ref, qs_ref, kc_ref, ks_ref, q_ref, k_ref, v_ref):
    xn = _rms(h_ref[...], g_ref[...]).astype(BF16)
    c = jnp.dot(xn, win_ref[...], preferred_element_type=F32)
    cq = _rms(c[:, :B_Q_RANK], qn_ref[...]).astype(BF16)
    ckv = _rms(c[:, B_Q_RANK:B_Q_RANK + B_KV_RANK], kvn_ref[...]).astype(BF16)
    kpe = _mla_rope(c[:, B_Q_RANK + B_KV_RANK:], kc_ref[...], ks_ref[...])
    q = jnp.dot(cq, wq_ref[...], preferred_element_type=F32)
    k = jnp.dot(ckv, wk_ref[...], preferred_element_type=F32)
    v = jnp.dot(ckv, wv_ref[...], preferred_element_type=F32)
    qc, qs = qc_ref[...], qs_ref[...]
    ones_col = (lax.broadcasted_iota(jnp.int32, kpe.shape, 1) == B_V).astype(F32)
    for hd in range(B_HEADS):
        sl = slice(hd * B_HEAD_PAD, (hd + 1) * B_HEAD_PAD)
        q_ref[0, hd] = _mla_rope(q[:, sl], qc, qs).astype(BF16)
        k_ref[0, hd] = (k[:, sl] + kpe).astype(BF16)
        v_ref[0, hd] = (v[:, sl] + ones_col).astype(BF16)


def _mla_proj(h, g, win, qn, kvn, wq, wk, wv, tabs, batch, lp):
    tiles_per_batch = lp // ROW_TILE
    row = lambda b, i: (b * tiles_per_batch + i, 0)
    fixed = lambda b, i: (0, 0)
    tab = lambda b, i: (i, 0)
    head_out = pl.BlockSpec((1, B_HEADS, ROW_TILE, B_HEAD_PAD), lambda b, i: (b, 0, i, 0))
    hw = B_HEADS * B_HEAD_PAD
    shape = jax.ShapeDtypeStruct((batch, B_HEADS, lp, B_HEAD_PAD), BF16)
    return pl.pallas_call(
        _mla_proj_kernel,
        grid=(batch, tiles_per_batch),
        in_specs=[
            pl.BlockSpec((ROW_TILE, D_MODEL), row),
            pl.BlockSpec((1, D_MODEL), fixed),
            pl.BlockSpec((D_MODEL, 512), fixed),
            pl.BlockSpec((1, B_Q_RANK), fixed),
            pl.BlockSpec((1, B_KV_RANK), fixed),
            pl.BlockSpec((B_Q_RANK, hw), fixed),
            pl.BlockSpec((B_KV_RANK, hw), fixed),
            pl.BlockSpec((B_KV_RANK, hw), fixed),
        ] + [pl.BlockSpec((ROW_TILE, 128), tab)] * 4,
        out_specs=[head_out, head_out, head_out],
        out_shape=[shape, shape, shape],
        compiler_params=_params("parallel", "parallel"),
        name="mla_proj",
    )(h, g, win, qn, kvn, wq, wk, wv, *tabs)


def _mla_attn_kernel(q_ref, k_ref, v_ref, o_ref, m_sc, acc_sc, *, s_real, tq, tk):
    qi = pl.program_id(2)
    nq = s_real // tq
    is_meta_q = qi == nq
    nt = (((1,), (1,)), ((), ()))
    sub = min(MLA_SUB_ROWS, tq)
    n_sub = tq // sub

    m_sc[...] = jnp.full(m_sc.shape, NEG_INF, F32)
    acc_sc[...] = jnp.zeros(acc_sc.shape, F32)

    def item(r, key0, width, mask_fn, with_meta=False):
        rows = slice(r * sub, (r + 1) * sub)
        k = k_ref[0, 0, pl.ds(key0, width), :]
        if with_meta:
            k = jnp.concatenate([k_ref[0, 0, s_real:s_real + BLOCK, :], k], axis=0)
        s = lax.dot_general(q_ref[0, 0, rows, :], k, nt, preferred_element_type=F32)
        if mask_fn is not None:
            rr = lax.broadcasted_iota(jnp.int32, s.shape, 0) + r * sub
            cc = lax.broadcasted_iota(jnp.int32, s.shape, 1)
            s = jnp.where(mask_fn(rr, cc), s, NEG_INF)

        def finish():
            v = v_ref[0, 0, pl.ds(key0, width), :]
            if with_meta:
                v = jnp.concatenate([v_ref[0, 0, s_real:s_real + BLOCK, :], v], axis=0)
            m_old = m_sc[rows]
            m_new = jnp.maximum(m_old, jnp.max(s, axis=-1, keepdims=True))
            p = jnp.exp2(s - jnp.concatenate([m_new] * (s.shape[1] // 128), axis=1)).astype(BF16)
            acc_sc[rows] = acc_sc[rows] * jnp.exp2(m_old - m_new) + jnp.dot(p, v, preferred_element_type=F32)
            m_sc[rows] = m_new
        return finish

    def run(items):
        pending = None
        for make in items:
            nxt = make()
            if pending is not None:
                pending()
            pending = nxt
        pending()

    def full_body(j, carry):
        start = pl.multiple_of(j * tk, tk)
        run([functools.partial(item, r, start, tk, None) for r in range(n_sub)])
        return carry

    n_full = jnp.where(is_meta_q, 0, qi * (tq // tk))
    lax.fori_loop(0, n_full, full_body, 0)

    @pl.when(jnp.logical_not(is_meta_q))
    def _():
        base = pl.multiple_of(qi * tq, tq)
        diag_mask = lambda rr, cc: jnp.where(cc < BLOCK, cc, cc - BLOCK) <= jnp.where(cc < BLOCK, N_META - 1, rr)
        run([functools.partial(item, r, base, (r + 1) * sub, diag_mask, True) for r in range(n_sub)])
        acc = acc_sc[...]
        o_ref[0] = (acc * (1.0 / acc[:, B_V:B_V + 1])).astype(BF16)

    @pl.when(is_meta_q)
    def _():
        rows = min(sub, BLOCK)
        item(0, s_real, BLOCK, lambda rr, cc: (cc <= rr) & (cc < N_META))()
        acc = acc_sc[0:rows]
        o_ref[0, 0:rows] = (acc * (1.0 / acc[:, B_V:B_V + 1])).astype(BF16)


def _mla_attn(q, k, v, batch, lp, s_real):
    tq = min(MLA_Q_TILE, s_real)
    tk = min(MLA_K_CHUNK, tq)
    nq = s_real // tq
    qmap = lambda b, h, i: (b, h, i, 0)
    kvmap = lambda b, h, i: (b, h, 0, 0)
    return pl.pallas_call(
        functools.partial(_mla_attn_kernel, s_real=s_real, tq=tq, tk=tk),
        grid=(batch, B_HEADS, nq + 1),
        in_specs=[
            pl.BlockSpec((1, 1, tq, B_HEAD_PAD), qmap),
            pl.BlockSpec((1, 1, lp, B_HEAD_PAD), kvmap),
            pl.BlockSpec((1, 1, lp, B_HEAD_PAD), kvmap),
        ],
        out_specs=pl.BlockSpec((1, tq, B_HEAD_PAD), lambda b, h, i: (b, i, h)),
        out_shape=jax.ShapeDtypeStruct((batch, lp, B_HEADS * B_HEAD_PAD), BF16),
        scratch_shapes=[pltpu.VMEM((tq, 128), F32), pltpu.VMEM((tq, B_HEAD_PAD), F32)],
        compiler_params=_params("parallel", "parallel", "arbitrary"),
        name="mla_attn",
    )(q, k, v)


def _positions(lp, s_real):
    r = jnp.arange(lp, dtype=jnp.int32)
    return jnp.where(r < s_real, r + N_META, r - s_real).astype(F32)


def _rope_angles(pos, dim):
    inv_freq = 1.0 / (ROPE_THETA ** (jnp.arange(0, dim, 2, dtype=F32) / dim))
    ang = pos[:, None] * inv_freq[None, :]
    return jnp.cos(ang), jnp.sin(ang)


def _swa_tables(pos):
    cos, sin = _rope_angles(pos, A_HEAD_DIM)
    cos_t = jnp.tile(cos, (1, 4))
    sin_t = jnp.tile(jnp.concatenate([-sin, sin], axis=1), (1, 2))
    return cos_t, sin_t


def _mla_tables(pos, scale):
    cos, sin = _rope_angles(pos, B_ROPE)
    n = pos.shape[0]
    tail = jnp.zeros((n, 128 - B_NOPE - B_ROPE), F32)
    ones = jnp.ones((n, B_NOPE), F32)
    zeros = jnp.zeros((n, B_NOPE), F32)
    cos_t = jnp.concatenate([ones, cos, cos, tail], axis=1) * scale
    sin_t = jnp.concatenate([zeros, -sin, sin, tail], axis=1) * scale
    return cos_t, sin_t


def _pad_heads(w, n_heads, width, pad_to):
    k = w.shape[0]
    w = w.reshape(k, n_heads, width)
    w = jnp.pad(w, ((0, 0), (0, 0), (0, pad_to - width)))
    return w.reshape(k, n_heads * pad_to)


def kernel(x, meta_tokens, attn_norm, ffn_norm, final_norm, a_w_qkv, a_b_qkv, a_sinks, a_w_o, a_b_o,
           b_w_in, b_q_norm, b_kv_norm, b_w_uq, b_w_ukv, b_w_o,
           moe_w_group, moe_b_group, moe_w_router, moe_b_router, moe_w_gate, moe_w_up, moe_w_down):
    batch, s_real, _ = x.shape
    lp = s_real + BLOCK
    tp = batch * lp
    assert tp % ROW_TILE == 0 and lp % ROW_TILE == 0 and s_real % BLOCK == 0

    meta = jnp.broadcast_to(meta_tokens[None].astype(x.dtype), (batch, N_META, D_MODEL))
    pad = jnp.zeros((batch, BLOCK - N_META, D_MODEL), x.dtype)
    h = jnp.concatenate([x, meta, pad], axis=1).reshape(tp, D_MODEL)
    pos = _positions(lp, s_real)

    def router_params(i):
        wr = jnp.concatenate([moe_w_router[i], moe_w_group[i]], axis=1)
        wr = jnp.pad(wr, ((0, 0), (0, ROUTE_LANES - wr.shape[1])))
        br = jnp.concatenate([moe_b_router[i], moe_b_group[i]])
        br = jnp.pad(br, (0, ROUTE_LANES - br.shape[0]))[None]
        w_hi = wr.astype(BF16)
        w_lo = (wr - w_hi.astype(F32)).astype(BF16)
        return jnp.concatenate([w_hi, w_lo], axis=1), br

    w_gate = moe_w_gate.reshape(-1, D_MODEL, D_EXPERT)
    w_up = moe_w_up.reshape(-1, D_MODEL, D_EXPERT)
    w_down = moe_w_down.reshape(-1, D_EXPERT, D_MODEL)

    def moe(i, h1, route_t, final):
        return _moe_layer(h1, route_t, ffn_norm[i][None], w_gate, w_up, w_down, i,
                          final_norm[None], batch, lp, s_real, final)

    wq, wk, wv = a_w_qkv[0][:, :A_Q_W], a_w_qkv[0][:, A_Q_W:A_Q_W + A_KV_W], a_w_qkv[0][:, A_Q_W + A_KV_W:]
    bq, bk, bv = a_b_qkv[0][:A_Q_W], a_b_qkv[0][A_Q_W:A_Q_W + A_KV_W], a_b_qkv[0][A_Q_W + A_KV_W:]
    swap = lambda t: jnp.concatenate([t[..., A_HEAD_DIM:], t[..., :A_HEAD_DIM]], axis=-1)
    w_a = jnp.concatenate([wq, wk, swap(wk), wv, swap(wv)], axis=1).astype(BF16)
    b_a = jnp.concatenate([bq, bk, swap(bk), bv, swap(bv)])[None]
    cos_a, sin_a = _swa_tables(pos)
    q, kk, vv = _swa_qkv(h, attn_norm[0][None], w_a, b_a, cos_a, sin_a, lp)
    sink_rows = jnp.repeat(a_sinks[0].astype(F32).reshape(A_KV_HEADS, 4, 2).transpose(0, 2, 1).reshape(4, 4), BLOCK, axis=1)[:, None, :]
    o = _swa_attn(q, kk, vv, sink_rows, batch, lp, s_real)
    wr, br = router_params(0)
    h1, route_t = _attn_out(o, a_w_o[0].astype(BF16), a_b_o[0][None], h, ffn_norm[0][None], wr, br)
    h = moe(0, h1, route_t, False).reshape(tp, D_MODEL)

    scale = (B_NOPE + B_ROPE) ** -0.5
    w_in = b_w_in[0]
    half = B_ROPE // 2
    w_pe = w_in[:, B_Q_RANK + B_KV_RANK:]
    kpe_cols = jnp.pad(jnp.concatenate([w_pe, w_pe[:, :half]], axis=1), ((0, 0), (B_NOPE, 128 - B_NOPE - B_ROPE - half)))
    w_in_p = jnp.concatenate([w_in[:, :B_Q_RANK + B_KV_RANK], kpe_cols], axis=1).astype(BF16)
    w_uq = b_w_uq[0].reshape(B_Q_RANK, B_HEADS, B_NOPE + B_ROPE)
    w_uq = jnp.concatenate([w_uq, w_uq[:, :, B_NOPE:B_NOPE + half]], axis=2).reshape(B_Q_RANK, -1)
    w_q_p = _pad_heads(w_uq, B_HEADS, B_NOPE + B_ROPE + half, B_HEAD_PAD).astype(BF16)
    w_ukv = b_w_ukv[0].reshape(B_KV_RANK, B_HEADS, B_NOPE + B_V)
    w_k_p = _pad_heads(w_ukv[:, :, :B_NOPE].reshape(B_KV_RANK, -1), B_HEADS, B_NOPE, B_HEAD_PAD).astype(BF16)
    w_v_p = _pad_heads(w_ukv[:, :, B_NOPE:].reshape(B_KV_RANK, -1), B_HEADS, B_V, B_HEAD_PAD).astype(BF16)
    tabs = _mla_tables(pos, scale * math.log2(math.e)) + _mla_tables(pos, 1.0)
    q, k, v = _mla_proj(h, attn_norm[1][None], w_in_p, b_q_norm[0][None], b_kv_norm[0][None],
                        w_q_p, w_k_p, w_v_p, tabs, batch, lp)
    o = _mla_attn(q, k, v, batch, lp, s_real).reshape(tp, B_HEADS * B_HEAD_PAD)
    w_o_p = jnp.pad(b_w_o[0].reshape(B_HEADS, B_V, D_MODEL), ((0, 0), (0, B_HEAD_PAD - B_V), (0, 0)))
    w_o_p = w_o_p.reshape(B_HEADS * B_HEAD_PAD, D_MODEL).astype(BF16)
    wr, br = router_params(1)
    h1, route_t = _attn_out(o, w_o_p, jnp.zeros((1, D_MODEL), F32), h, ffn_norm[1][None], wr, br)
    return moe(1, h1, route_t, True)
```

```python
import functools
import math

import jax
import jax.numpy as jnp
from jax import lax
from jax.experimental import pallas as pl
from jax.experimental.pallas import tpu as pltpu

F32 = jnp.float32
BF16 = jnp.bfloat16

D_MODEL = 1024
N_META = 16
BLOCK = 128
ROPE_THETA = 10000.0
NORM_EPS = 1e-6
NEG_INF = -1e30

A_HEADS = 16
A_KV_HEADS = 2
A_HEAD_DIM = 64
A_Q_W = A_HEADS * A_HEAD_DIM
A_KV_W = A_KV_HEADS * A_HEAD_DIM

B_HEADS = 16
B_NOPE = 64
B_ROPE = 32
B_V = 64
B_Q_RANK = 256
B_KV_RANK = 128
B_HEAD_PAD = 128

N_GROUPS = 4
EXPERTS_PER_GROUP = 8
N_EXPERTS = N_GROUPS * EXPERTS_PER_GROUP
TOP_K = 2
D_EXPERT = 256
MOE_BLOCK = 128
ROUTE_LANES = 128

ROW_TILE = 640
MOE_STEP_BLOCKS = 4
DMA_UNROLL = 8
MLA_Q_TILE = 2048
MLA_K_CHUNK = 1024
MLA_SUB_ROWS = 256
VMEM_LIMIT = 56 * 1024 * 1024


def _rms(x, g):
    return x * lax.rsqrt(jnp.mean(x * x, axis=-1, keepdims=True) + NORM_EPS) * g


def _params(*sem):
    return pltpu.CompilerParams(dimension_semantics=sem, vmem_limit_bytes=VMEM_LIMIT)


def _swa_qkv_kernel(h_ref, g_ref, w_ref, b_ref, cos_ref, sin_ref, q_ref, k_ref, v_ref):
    xn = _rms(h_ref[...], g_ref[...]).astype(BF16)
    y = jnp.dot(xn, w_ref[...], preferred_element_type=F32) + b_ref[...]
    cos = cos_ref[...]
    sin = sin_ref[...]
    lane = lax.broadcasted_iota(jnp.int32, cos.shape, 1)
    first = (lane & (A_HEAD_DIM - 1)) < (A_HEAD_DIM // 2)

    def rope(c):
        rot = jnp.where(first, pltpu.roll(c, 128 - A_HEAD_DIM // 2, 1), pltpu.roll(c, A_HEAD_DIM // 2, 1))
        return c * cos + rot * sin

    for j in range(A_Q_W // 128):
        q_ref[:, j * 128:(j + 1) * 128] = (rope(y[:, j * 128:(j + 1) * 128]) * (A_HEAD_DIM ** -0.5)).astype(BF16)
    k_ref[:, 0:128] = rope(y[:, A_Q_W:A_Q_W + 128]).astype(BF16)
    k_ref[:, 128:256] = rope(y[:, A_Q_W + 128:A_Q_W + 256]).astype(BF16)
    v_ref[...] = y[:, A_Q_W + 256:A_Q_W + 512].astype(BF16)


def _swa_qkv(h, g, w, b, cos, sin, lp):
    tp = h.shape[0]
    tiles_per_batch = lp // ROW_TILE
    n_out = w.shape[1]
    row = lambda i: (i, 0)
    fixed = lambda i: (0, 0)
    tab = lambda i: (i % tiles_per_batch, 0)
    return pl.pallas_call(
        _swa_qkv_kernel,
        grid=(tp // ROW_TILE,),
        in_specs=[
            pl.BlockSpec((ROW_TILE, D_MODEL), row),
            pl.BlockSpec((1, D_MODEL), fixed),
            pl.BlockSpec((D_MODEL, n_out), fixed),
            pl.BlockSpec((1, n_out), fixed),
            pl.BlockSpec((ROW_TILE, 128), tab),
            pl.BlockSpec((ROW_TILE, 128), tab),
        ],
        out_specs=[
            pl.BlockSpec((ROW_TILE, A_Q_W), row),
            pl.BlockSpec((ROW_TILE, 256), row),
            pl.BlockSpec((ROW_TILE, 256), row),
        ],
        out_shape=[
            jax.ShapeDtypeStruct((tp, A_Q_W), BF16),
            jax.ShapeDtypeStruct((tp, 256), BF16),
            jax.ShapeDtypeStruct((tp, 256), BF16),
        ],
        compiler_params=_params("parallel"),
        name="swa_qkv",
    )(h, g, w, b, cos, sin)


def _swa_attn_kernel(q_ref, kc_ref, kp_ref, km_ref, vc_ref, vp_ref, vm_ref, sink_ref, o_ref, *, n_real_blocks):
    n = pl.program_id(1)
    is_real = n < n_real_blocks
    n_keys = N_META + 2 * BLOCK
    big = jnp.int32(1 << 20)
    key = lax.broadcasted_iota(jnp.int32, (n_keys, BLOCK), 0)
    qry = lax.broadcasted_iota(jnp.int32, (n_keys, BLOCK), 1)
    jp = key - N_META
    jc = key - (N_META + BLOCK)
    meta_lim = jnp.where(is_real, N_META, 0)
    prev_off = jnp.where(jnp.logical_and(is_real, n >= 1), 0, big)
    cur_lim = jnp.where(is_real, big, N_META)
    ok = (key < meta_lim) | ((jc < 0) & (jp > qry + prev_off)) | ((jc >= 0) & (jc <= qry) & (jc < cur_lim))
    bias1 = jnp.where(ok, 0.0, NEG_INF).astype(F32)
    bias = jnp.concatenate([bias1] * 4, axis=1)

    kcat = jnp.concatenate([km_ref[0], kp_ref[0], kc_ref[0]], axis=0)
    vcat = jnp.concatenate([vm_ref[0], vp_ref[0], vc_ref[0]], axis=0)
    lo = lax.broadcasted_iota(jnp.int32, (n_keys, 128), 1) < A_HEAD_DIM
    zero = jnp.zeros((n_keys, 128), BF16)

    def scores(kvh, parity):
        a, b = (kcat[:, :128], kcat[:, 128:]) if kvh == 0 else (kcat[:, 128:], kcat[:, :128])
        kx = jnp.where(lo, a, zero) if parity == 0 else jnp.where(lo, zero, b)
        base = kvh * (A_Q_W // A_KV_HEADS)
        qs = jnp.concatenate([q_ref[0, :, base + c * 128:base + (c + 1) * 128] for c in range(4)], axis=0)
        st = lax.dot_general(kx, qs, (((1,), (1,)), ((), ())), preferred_element_type=F32) + bias

        def finish():
            a, b = (vcat[:, :128], vcat[:, 128:]) if kvh == 0 else (vcat[:, 128:], vcat[:, :128])
            vx = jnp.where(lo, a, zero) if parity == 0 else jnp.where(lo, zero, b)
            sink = sink_ref[kvh * 2 + parity]
            m = jnp.maximum(jnp.max(st, axis=0, keepdims=True), sink)
            p = jnp.exp(st - m)
            denom = jnp.sum(p, axis=0, keepdims=True) + jnp.exp(sink - m)
            ot = lax.dot_general(vx, p.astype(BF16), (((0,), (0,)), ((), ())), preferred_element_type=F32)
            return ot * (1.0 / denom)
        return finish

    for kvh in range(A_KV_HEADS):
        base = kvh * (A_Q_W // A_KV_HEADS)
        even = scores(kvh, 0)
        odd = scores(kvh, 1)
        ot = even() + odd()
        for c in range(4):
            o_ref[0, :, base + c * 128:base + (c + 1) * 128] = ot[:, c * BLOCK:(c + 1) * BLOCK].T.astype(BF16)


def _swa_attn(q, kk, vv, sink_rows, batch, lp, s_real):
    nb = s_real // BLOCK
    q3 = q.reshape(batch, lp, A_Q_W)
    k3 = kk.reshape(batch, lp, 256)
    v3 = vv.reshape(batch, lp, 256)
    cur = lambda b, n: (b, n, 0)
    prev = lambda b, n: (b, jnp.maximum(n - 1, 0), 0)
    meta = lambda b, n: (b, s_real // N_META, 0)
    out = pl.pallas_call(
        functools.partial(_swa_attn_kernel, n_real_blocks=nb),
        grid=(batch, nb + 1),
        in_specs=[
            pl.BlockSpec((1, BLOCK, A_Q_W), cur),
            pl.BlockSpec((1, BLOCK, 256), cur),
            pl.BlockSpec((1, BLOCK, 256), prev),
            pl.BlockSpec((1, N_META, 256), meta),
            pl.BlockSpec((1, BLOCK, 256), cur),
            pl.BlockSpec((1, BLOCK, 256), prev),
            pl.BlockSpec((1, N_META, 256), meta),
            pl.BlockSpec((4, 1, 4 * BLOCK), lambda b, n: (0, 0, 0)),
        ],
        out_specs=pl.BlockSpec((1, BLOCK, A_Q_W), cur),
        out_shape=jax.ShapeDtypeStruct((batch, lp, A_Q_W), BF16),
        compiler_params=_params("parallel", "parallel"),
        name="swa_attn",
    )(q3, k3, k3, k3, v3, v3, v3, sink_rows)
    return out.reshape(batch * lp, A_Q_W)


ROUTE_ROWS = 40


def _route(xn, wr_ref, br):
    x_hi = xn.astype(BF16)
    x_lo = (xn - x_hi.astype(F32)).astype(BF16)
    both = jnp.dot(x_hi, wr_ref[...], preferred_element_type=F32)
    lg = (both[:, :ROUTE_LANES] + both[:, ROUTE_LANES:]
          + jnp.dot(x_lo, wr_ref[:, :ROUTE_LANES], preferred_element_type=F32) + br)
    lt = lg.T[0:ROUTE_ROWS]
    row_i = lax.broadcasted_iota(jnp.int32, lt.shape, 0)
    row = row_i.astype(F32)
    row_grp = (row_i >> 3).astype(F32)
    big = 1e9
    is_g = jnp.logical_and(row_i >= N_EXPERTS, row_i < N_EXPERTS + N_GROUPS)
    gl = jnp.where(is_g, lt, -jnp.inf)
    gmax = jnp.max(gl, axis=0, keepdims=True)
    g_p = 1.0 / jnp.sum(jnp.exp(gl - gmax), axis=0, keepdims=True)
    g_idx = jnp.min(jnp.where(gl == gmax, row - float(N_EXPERTS), big), axis=0, keepdims=True)
    el = jnp.where(row_grp == g_idx, lt, -jnp.inf)
    m1 = jnp.max(el, axis=0, keepdims=True)
    esum = jnp.sum(jnp.exp(el - m1), axis=0, keepdims=True)
    i1 = jnp.min(jnp.where(el == m1, row, big), axis=0, keepdims=True)
    el2 = jnp.where(row == i1, -jnp.inf, el)
    m2 = jnp.max(el2, axis=0, keepdims=True)
    i2 = jnp.min(jnp.where(el2 == m2, row, big), axis=0, keepdims=True)
    e1 = 1.0 / esum
    e2 = jnp.exp(m2 - m1) / esum
    w1 = g_p * e1 / (e1 + e2)
    w2 = g_p * e2 / (e1 + e2)
    return jnp.concatenate([i1, i2, w1, w2, jnp.zeros((4, lt.shape[1]), F32)], axis=0)


def _attn_out_kernel(o_ref, w_ref, b_ref, h_ref, g_ref, wr_ref, br_ref, h1_ref, route_t_ref):
    h1 = h_ref[...] + jnp.dot(o_ref[...], w_ref[...], preferred_element_type=F32) + b_ref[...]
    h1_ref[...] = h1
    route_t_ref[...] = _route(_rms(h1, g_ref[...]), wr_ref, br_ref[...])


def _attn_out(o, w, b, h, g, wr, br):
    tp, ko = o.shape
    row = lambda i: (i, 0)
    fixed = lambda i: (0, 0)
    return pl.pallas_call(
        _attn_out_kernel,
        grid=(tp // ROW_TILE,),
        in_specs=[
            pl.BlockSpec((ROW_TILE, ko), row),
            pl.BlockSpec((ko, D_MODEL), fixed),
            pl.BlockSpec((1, D_MODEL), fixed),
            pl.BlockSpec((ROW_TILE, D_MODEL), row),
            pl.BlockSpec((1, D_MODEL), fixed),
            pl.BlockSpec((D_MODEL, 2 * ROUTE_LANES), fixed),
            pl.BlockSpec((1, ROUTE_LANES), fixed),
        ],
        out_specs=[
            pl.BlockSpec((ROW_TILE, D_MODEL), row),
            pl.BlockSpec((8, ROW_TILE), lambda i: (0, i)),
        ],
        out_shape=[
            jax.ShapeDtypeStruct((tp, D_MODEL), F32),
            jax.ShapeDtypeStruct((8, tp), F32),
        ],
        compiler_params=_params("parallel"),
        name="attn_out_route",
    )(o, w, b, h, g, wr, br)


def _plan_kernel(rt_ref, pos_ref, be_ref, pends_ref, cnt_col, carry, *, n_blocks_pad):
    phase = pl.program_id(0)
    i = pl.program_id(1)
    tm = rt_ref.shape[1]
    id0 = rt_ref[0:1, :]
    id1 = rt_ref[1:2, :]
    e_sub = lax.broadcasted_iota(jnp.int32, (N_EXPERTS, tm), 0).astype(F32)
    hit0 = e_sub == id0
    hit1 = e_sub == id1
    member_t = jnp.where(hit0, 1.0, jnp.where(hit1, 1.0, 0.0))
    tile_cnt_col = jnp.sum(member_t, axis=1, keepdims=True)

    @pl.when(jnp.logical_and(phase == 0, i == 0))
    def _():
        cnt_col[...] = jnp.zeros(cnt_col.shape, F32)

    @pl.when(phase == 0)
    def _():
        cnt_col[...] += jnp.broadcast_to(tile_cnt_col, cnt_col.shape)

    @pl.when(jnp.logical_and(phase == 1, i == 0))
    def _():
        pad_to_block = lambda c: jnp.floor((c + (MOE_BLOCK - 1.0)) * (1.0 / MOE_BLOCK)) * MOE_BLOCK
        padded_col = pad_to_block(cnt_col[...])
        square = jnp.concatenate([padded_col, jnp.zeros((128 - N_EXPERTS, 128), F32)], axis=0)
        padded_row = square.T[0:1]
        lane_e = lax.broadcasted_iota(jnp.int32, (N_EXPERTS, 128), 1)
        sub_e = lax.broadcasted_iota(jnp.int32, (N_EXPERTS, 128), 0)
        pstart_col = jnp.sum(jnp.where(lane_e < sub_e, padded_row, 0.0), axis=1, keepdims=True)
        pends_row = jnp.sum(jnp.where(sub_e <= lane_e, padded_col, 0.0), axis=0, keepdims=True)
        pends_col = pstart_col + padded_col[:, 0:1]
        carry[...] = jnp.broadcast_to(pstart_col, carry.shape)
        pends_ref[...] = pends_row.astype(jnp.int32)
        blk_lane = lax.broadcasted_iota(jnp.int32, (N_EXPERTS, n_blocks_pad), 1)
        blk_start = (blk_lane * MOE_BLOCK).astype(F32)
        be = jnp.sum(jnp.where(pends_col <= blk_start, 1.0, 0.0), axis=0, keepdims=True)
        be = jnp.minimum(be, N_EXPERTS - 1.0)
        n_used = pends_row[:, N_EXPERTS - 1:N_EXPERTS] * (1.0 / MOE_BLOCK)
        be = jnp.where(blk_lane[0:1] == n_blocks_pad - 1, n_used, be)
        be_ref[...] = be.astype(jnp.int32)

    @pl.when(phase == 1)
    def _():
        before = lax.broadcasted_iota(jnp.int32, (tm, tm), 0) < lax.broadcasted_iota(jnp.int32, (tm, tm), 1)
        prefix = jnp.dot(member_t.astype(BF16), jnp.where(before, 1.0, 0.0).astype(BF16),
                         preferred_element_type=F32)
        row_of = prefix + carry[:, 0:1]
        dest0 = jnp.sum(jnp.where(hit0, row_of, 0.0), axis=0, keepdims=True)
        dest1 = jnp.sum(jnp.where(hit1, row_of, 0.0), axis=0, keepdims=True)
        pos_ref[...] = jnp.concatenate([dest0, dest1], axis=0).astype(jnp.int32)
        carry[...] += jnp.broadcast_to(tile_cnt_col, carry.shape)


def _dispatch_plan(route_t, n_blocks):
    tp = route_t.shape[1]
    n_tiles = tp // ROW_TILE
    n_blocks_pad = -(-(n_blocks + 1) // 128) * 128
    pos, be, pends = pl.pallas_call(
        functools.partial(_plan_kernel, n_blocks_pad=n_blocks_pad),
        grid=(2, n_tiles),
        in_specs=[
            pl.BlockSpec((8, ROW_TILE), lambda p, i: (0, i)),
        ],
        out_specs=[
            pl.BlockSpec((TOP_K, ROW_TILE), lambda p, i: (0, i * p)),
            pl.BlockSpec((1, n_blocks_pad), lambda p, i: (0, 0)),
            pl.BlockSpec((1, 128), lambda p, i: (0, 0)),
        ],
        out_shape=[
            jax.ShapeDtypeStruct((TOP_K, tp), jnp.int32),
            jax.ShapeDtypeStruct((1, n_blocks_pad), jnp.int32),
            jax.ShapeDtypeStruct((1, 128), jnp.int32),
        ],
        scratch_shapes=[pltpu.VMEM((N_EXPERTS, 128), F32), pltpu.VMEM((N_EXPERTS, 128), F32)],
        compiler_params=_params("arbitrary", "arbitrary"),
        name="moe_plan",
    )(route_t)
    return pos.reshape(-1), be.reshape(-1), pends.reshape(-1)


def _scatter_row(stage, slot, r, buf_hbm, dst_row, sem):
    return pltpu.make_async_copy(stage.at[slot, pl.ds(r, 1)], buf_hbm.at[pl.ds(dst_row, 1)], sem.at[slot])


def _dispatch_kernel(pos_ref, pends_ref, h_ref, buf_hbm, stage, zsem, sem, *, n_tokens, n_blocks):
    i = pl.program_id(0)
    slot = i % 2

    def zero_block(row0):
        return pltpu.make_async_copy(stage.at[0], buf_hbm.at[pl.ds(pl.multiple_of(row0, MOE_BLOCK), MOE_BLOCK)], zsem)

    def zero_tail(e):
        return zero_block(jnp.maximum(pends_ref[e] - MOE_BLOCK, 0))

    @pl.when(i == 0)
    def _():
        stage[0] = jnp.zeros((BLOCK, D_MODEL), F32)
        for e in range(N_EXPERTS):
            zero_tail(e).start()
        for e in range(N_EXPERTS):
            zero_tail(e).wait()
        first_unused = pends_ref[N_EXPERTS - 1] // MOE_BLOCK

        def start_unused(b, carry):
            zero_block(b * MOE_BLOCK).start()
            return carry

        def wait_unused(b, carry):
            zero_block(b * MOE_BLOCK).wait()
            return carry
        lax.fori_loop(first_unused, n_blocks, start_unused, 0)
        lax.fori_loop(first_unused, n_blocks, wait_unused, 0)

    stage[slot] = h_ref[...]

    def issue(c, carry):
        for u in range(DMA_UNROLL):
            r = pl.multiple_of(c * DMA_UNROLL, DMA_UNROLL) + u
            t = i * BLOCK + r
            _scatter_row(stage, slot, r, buf_hbm, pos_ref[t], sem).start()
            _scatter_row(stage, slot, r, buf_hbm, pos_ref[n_tokens + t], sem).start()
        return carry
    lax.fori_loop(0, BLOCK // DMA_UNROLL, issue, 0)

    def drain(which):
        for _ in range(2 * BLOCK):
            _scatter_row(stage, which, 0, buf_hbm, 0, sem).wait()

    @pl.when(i > 0)
    def _():
        drain(1 - slot)

    @pl.when(i == pl.num_programs(0) - 1)
    def _():
        drain(slot)


def _moe_dispatch(h1, pos, pends, n_rows):
    tp = h1.shape[0]
    grid_spec = pltpu.PrefetchScalarGridSpec(
        num_scalar_prefetch=2,
        grid=(tp // BLOCK,),
        in_specs=[pl.BlockSpec((BLOCK, D_MODEL), lambda i, pos, pends: (i, 0))],
        out_specs=pl.BlockSpec(memory_space=pl.ANY),
        scratch_shapes=[
            pltpu.VMEM((2, BLOCK, D_MODEL), F32),
            pltpu.SemaphoreType.DMA(()),
            pltpu.SemaphoreType.DMA((2,)),
        ],
    )
    return pl.pallas_call(
        functools.partial(_dispatch_kernel, n_tokens=tp, n_blocks=n_rows // MOE_BLOCK),
        grid_spec=grid_spec,
        out_shape=jax.ShapeDtypeStruct((n_rows, D_MODEL), F32),
        compiler_params=_params("arbitrary"),
        name="moe_dispatch",
    )(pos, pends, h1)


def _moe_kernel(be_ref, x_ref, g_ref, *refs, n_blocks_pad):
    nb = MOE_STEP_BLOCKS
    weights = [refs[3 * j:3 * j + 3] for j in range(nb)]
    y_ref, wg_bf, wu_bf, wd_bf = refs[3 * nb:]
    i = pl.program_id(0)
    n_used = be_ref[n_blocks_pad - 1]

    def expert_of(blk):
        return be_ref[jnp.clip(blk, 0, n_used - 1)]

    for half, (wg_ref, wu_ref, wd_ref) in enumerate(weights):
        blk = nb * i + half

        @pl.when(jnp.logical_or(i == 0, expert_of(blk) != expert_of(blk - nb)))
        def _():
            wg_bf[half] = wg_ref[0].astype(BF16)
            wu_bf[half] = wu_ref[0].astype(BF16)
            wd_bf[half] = wd_ref[0].astype(BF16)

    def rows(half):
        return slice(half * MOE_BLOCK, (half + 1) * MOE_BLOCK)

    def gate_up(half):
        xn = _rms(x_ref[rows(half), :], g_ref[...]).astype(BF16)
        return (jnp.dot(xn, wg_bf[half], preferred_element_type=F32),
                jnp.dot(xn, wu_bf[half], preferred_element_type=F32))

    def down(half, gate, up):
        act = (gate * jax.nn.sigmoid(gate) * up).astype(BF16)
        y_ref[rows(half), :] = jnp.dot(act, wd_bf[half], preferred_element_type=F32)

    def zero(half):
        y_ref[rows(half), :] = jnp.zeros((MOE_BLOCK, D_MODEL), F32)

    @pl.when(nb * i + nb - 1 < n_used)
    def _():
        pending = None
        for j in range(nb):
            nxt = gate_up(j)
            if pending is not None:
                down(j - 1, *pending)
            pending = nxt
        down(nb - 1, *pending)

    @pl.when(nb * i + nb - 1 >= n_used)
    def _():
        for j in range(nb):
            @pl.when(nb * i + j < n_used)
            def _():
                down(j, *gate_up(j))

            @pl.when(nb * i + j >= n_used)
            def _():
                zero(j)


def _moe_experts(buf, g, wg, wu, wd, layer, block_expert, n_blocks):
    n_blocks_pad = block_expert.shape[0]
    nb = MOE_STEP_BLOCKS
    assert n_blocks % nb == 0
    last = lambda be: be[n_blocks_pad - 1] - 1
    group = lambda i, be: (jnp.minimum(i, last(be) // nb), 0)
    ew = lambda j: (lambda i, be: (layer * N_EXPERTS + be[jnp.minimum(nb * i + j, last(be))], 0, 0))
    weights = lambda j: [
        pl.BlockSpec((1, D_MODEL, D_EXPERT), ew(j)),
        pl.BlockSpec((1, D_MODEL, D_EXPERT), ew(j)),
        pl.BlockSpec((1, D_EXPERT, D_MODEL), ew(j)),
    ]
    grid_spec = pltpu.PrefetchScalarGridSpec(
        num_scalar_prefetch=1,
        grid=(n_blocks // nb,),
        in_specs=[
            pl.BlockSpec((nb * MOE_BLOCK, D_MODEL), group),
            pl.BlockSpec((1, D_MODEL), lambda i, be: (0, 0)),
        ] + [spec for j in range(nb) for spec in weights(j)],
        out_specs=pl.BlockSpec((nb * MOE_BLOCK, D_MODEL), lambda i, be: (i, 0)),
        scratch_shapes=[
            pltpu.VMEM((nb, D_MODEL, D_EXPERT), BF16),
            pltpu.VMEM((nb, D_MODEL, D_EXPERT), BF16),
            pltpu.VMEM((nb, D_EXPERT, D_MODEL), BF16),
        ],
    )
    return pl.pallas_call(
        functools.partial(_moe_kernel, n_blocks_pad=n_blocks_pad),
        grid_spec=grid_spec,
        out_shape=jax.ShapeDtypeStruct((n_blocks * MOE_BLOCK, D_MODEL), F32),
        compiler_params=_params("arbitrary"),
        name="moe_experts",
    )(block_expert, buf, g, *([wg, wu, wd] * nb))


def _row_copy(src_hbm, src_row, dst, slot, dst_row, sem):
    return pltpu.make_async_copy(src_hbm.at[pl.ds(src_row, 1)], dst.at[slot, pl.ds(dst_row, 1)], sem.at[slot])


def _combine_kernel(pos_ref, h_ref, rt_ref, g_ref, y_hbm, o_ref, buf, sem, *, tiles_per_batch, n_tokens, final):
    b = pl.program_id(0)
    j = pl.program_id(1)
    nj = pl.num_programs(1)
    step = b * nj + j
    slot = step % 2

    def issue(bb, jj, to_slot):
        tile = bb * tiles_per_batch + jj
        def body(c, carry):
            for u in range(DMA_UNROLL):
                r = pl.multiple_of(c * DMA_UNROLL, DMA_UNROLL) + u
                t = tile * BLOCK + r
                _row_copy(y_hbm, pos_ref[t], buf, to_slot, r, sem).start()
                _row_copy(y_hbm, pos_ref[n_tokens + t], buf, to_slot, BLOCK + r, sem).start()
            return carry
        lax.fori_loop(0, BLOCK // DMA_UNROLL, body, 0)

    @pl.when(step == 0)
    def _():
        issue(b, j, 0)

    @pl.when(step + 1 < pl.num_programs(0) * nj)
    def _():
        nxt = j + 1
        wrap = nxt == nj
        issue(jnp.where(wrap, b + 1, b), jnp.where(wrap, 0, nxt), 1 - slot)

    for _ in range(2 * BLOCK):
        _row_copy(y_hbm, 0, buf, slot, 0, sem).wait()

    w = jnp.concatenate([rt_ref[...], jnp.zeros((BLOCK - 8, BLOCK), F32)], axis=0).T
    h2 = h_ref[0] + w[:, TOP_K:TOP_K + 1] * buf[slot, 0:BLOCK] + w[:, TOP_K + 1:TOP_K + 2] * buf[slot, BLOCK:2 * BLOCK]
    if final:
        h2 = _rms(h2, g_ref[...])
    o_ref[0] = h2


def _moe_combine(h1, route_t, y, pos, g, batch, lp, s_real, final):
    tiles_per_batch = lp // BLOCK
    nj = s_real // BLOCK if final else tiles_per_batch
    out_rows = s_real if final else lp
    tile = lambda b, j, pos: (b, j, 0)
    grid_spec = pltpu.PrefetchScalarGridSpec(
        num_scalar_prefetch=1,
        grid=(batch, nj),
        in_specs=[
            pl.BlockSpec((1, BLOCK, D_MODEL), tile),
            pl.BlockSpec((8, BLOCK), lambda b, j, pos: (0, b * tiles_per_batch + j)),
            pl.BlockSpec((1, D_MODEL), lambda b, j, pos: (0, 0)),
            pl.BlockSpec(memory_space=pl.ANY),
        ],
        out_specs=pl.BlockSpec((1, BLOCK, D_MODEL), tile),
        scratch_shapes=[
            pltpu.VMEM((2, 2 * BLOCK, D_MODEL), F32),
            pltpu.SemaphoreType.DMA((2,)),
        ],
    )
    return pl.pallas_call(
        functools.partial(_combine_kernel, tiles_per_batch=tiles_per_batch, n_tokens=batch * lp, final=final),
        grid_spec=grid_spec,
        out_shape=jax.ShapeDtypeStruct((batch, out_rows, D_MODEL), F32),
        compiler_params=_params("arbitrary", "arbitrary"),
        name="moe_combine_final" if final else "moe_combine",
    )(pos, h1.reshape(batch, lp, D_MODEL), route_t, g, y)


def _moe_layer(h1, route_t, g, wg, wu, wd, layer, final_g, batch, lp, s_real, final):
    tp = batch * lp
    n_blocks = -(-(tp * TOP_K + N_EXPERTS * (MOE_BLOCK - 1)) // MOE_BLOCK)
    n_blocks = -(-n_blocks // MOE_STEP_BLOCKS) * MOE_STEP_BLOCKS
    pos, block_expert, pends = _dispatch_plan(route_t, n_blocks)
    buf = _moe_dispatch(h1, pos, pends, n_blocks * MOE_BLOCK)
    y = _moe_experts(buf, g, wg, wu, wd, layer, block_expert, n_blocks)
    return _moe_combine(h1, route_t, y, pos, final_g, batch, lp, s_real, final)


def _mla_rope(c, cos, sin):
    return c * cos + pltpu.roll(c, 128 - B_ROPE // 2, 1) * sin


def _mla_proj_kernel(h_ref, g_ref, win_ref, qn_ref, kvn_ref, wq_ref, wk_ref, wv_ref,
                     qc_ref, qs_ref, kc_ref, ks_ref, q_ref, k_ref, v_ref):
    xn = _rms(h_ref[...], g_ref[...]).astype(BF16)
    c = jnp.dot(xn, win_ref[...], preferred_element_type=F32)
    cq = _rms(c[:, :B_Q_RANK], qn_ref[...]).astype(BF16)
    ckv = _rms(c[:, B_Q_RANK:B_Q_RANK + B_KV_RANK], kvn_ref[...]).astype(BF16)
    kpe = _mla_rope(c[:, B_Q_RANK + B_KV_RANK:], kc_ref[...], ks_ref[...])
    q = jnp.dot(cq, wq_ref[...], preferred_element_type=F32)
    k = jnp.dot(ckv, wk_ref[...], preferred_element_type=F32)
    v = jnp.dot(ckv, wv_ref[...], preferred_element_type=F32)
    qc, qs = qc_ref[...], qs_ref[...]
    ones_col = (lax.broadcasted_iota(jnp.int32, kpe.shape, 1) == B_V).astype(F32)
    for hd in range(B_HEADS):
        sl = slice(hd * B_HEAD_PAD, (hd + 1) * B_HEAD_PAD)
        q_ref[0, hd] = _mla_rope(q[:, sl], qc, qs).astype(BF16)
        k_ref[0, hd] = (k[:, sl] + kpe).astype(BF16)
        v_ref[0, hd] = (v[:, sl] + ones_col).astype(BF16)


def _mla_proj(h, g, win, qn, kvn, wq, wk, wv, tabs, batch, lp):
    tiles_per_batch = lp // ROW_TILE
    row = lambda b, i: (b * tiles_per_batch + i, 0)
    fixed = lambda b, i: (0, 0)
    tab = lambda b, i: (i, 0)
    head_out = pl.BlockSpec((1, B_HEADS, ROW_TILE, B_HEAD_PAD), lambda b, i: (b, 0, i, 0))
    hw = B_HEADS * B_HEAD_PAD
    shape = jax.ShapeDtypeStruct((batch, B_HEADS, lp, B_HEAD_PAD), BF16)
    return pl.pallas_call(
        _mla_proj_kernel,
        grid=(batch, tiles_per_batch),
        in_specs=[
            pl.BlockSpec((ROW_TILE, D_MODEL), row),
            pl.BlockSpec((1, D_MODEL), fixed),
            pl.BlockSpec((D_MODEL, 512), fixed),
            pl.BlockSpec((1, B_Q_RANK), fixed),
            pl.BlockSpec((1, B_KV_RANK), fixed),
            pl.BlockSpec((B_Q_RANK, hw), fixed),
            pl.BlockSpec((B_KV_RANK, hw), fixed),
            pl.BlockSpec((B_KV_RANK, hw), fixed),
        ] + [pl.BlockSpec((ROW_TILE, 128), tab)] * 4,
        out_specs=[head_out, head_out, head_out],
        out_shape=[shape, shape, shape],
        compiler_params=_params("parallel", "parallel"),
        name="mla_proj",
    )(h, g, win, qn, kvn, wq, wk, wv, *tabs)


def _mla_attn_kernel(q_ref, k_ref, v_ref, o_ref, m_sc, acc_sc, *, s_real, tq, tk):
    qi = pl.program_id(2)
    nq = s_real // tq
    is_meta_q = qi == nq
    nt = (((1,), (1,)), ((), ()))
    sub = min(MLA_SUB_ROWS, tq)
    n_sub = tq // sub

    m_sc[...] = jnp.full(m_sc.shape, NEG_INF, F32)
    acc_sc[...] = jnp.zeros(acc_sc.shape, F32)

    def item(r, key0, width, mask_fn, with_meta=False):
        rows = slice(r * sub, (r + 1) * sub)
        k = k_ref[0, 0, pl.ds(key0, width), :]
        if with_meta:
            k = jnp.concatenate([k_ref[0, 0, s_real:s_real + BLOCK, :], k], axis=0)
        s = lax.dot_general(q_ref[0, 0, rows, :], k, nt, preferred_element_type=F32)
        if mask_fn is not None:
            rr = lax.broadcasted_iota(jnp.int32, s.shape, 0) + r * sub
            cc = lax.broadcasted_iota(jnp.int32, s.shape, 1)
            s = jnp.where(mask_fn(rr, cc), s, NEG_INF)

        def finish():
            v = v_ref[0, 0, pl.ds(key0, width), :]
            if with_meta:
                v = jnp.concatenate([v_ref[0, 0, s_real:s_real + BLOCK, :], v], axis=0)
            m_old = m_sc[rows]
            m_new = jnp.maximum(m_old, jnp.max(s, axis=-1, keepdims=True))
            p = jnp.exp2(s - jnp.concatenate([m_new] * (s.shape[1] // 128), axis=1)).astype(BF16)
            acc_sc[rows] = acc_sc[rows] * jnp.exp2(m_old - m_new) + jnp.dot(p, v, preferred_element_type=F32)
            m_sc[rows] = m_new
        return finish

    def run(items):
        pending = None
        for make in items:
            nxt = make()
            if pending is not None:
                pending()
            pending = nxt
        pending()

    def full_body(j, carry):
        start = pl.multiple_of(j * tk, tk)
        run([functools.partial(item, r, start, tk, None) for r in range(n_sub)])
        return carry

    n_full = jnp.where(is_meta_q, 0, qi * (tq // tk))
    lax.fori_loop(0, n_full, full_body, 0)

    @pl.when(jnp.logical_not(is_meta_q))
    def _():
        base = pl.multiple_of(qi * tq, tq)
        diag_mask = lambda rr, cc: jnp.where(cc < BLOCK, cc, cc - BLOCK) <= jnp.where(cc < BLOCK, N_META - 1, rr)
        run([functools.partial(item, r, base, (r + 1) * sub, diag_mask, True) for r in range(n_sub)])
        acc = acc_sc[...]
        o_ref[0] = (acc * (1.0 / acc[:, B_V:B_V + 1])).astype(BF16)

    @pl.when(is_meta_q)
    def _():
        rows = min(sub, BLOCK)
        item(0, s_real, BLOCK, lambda rr, cc: (cc <= rr) & (cc < N_META))()
        acc = acc_sc[0:rows]
        o_ref[0, 0:rows] = (acc * (1.0 / acc[:, B_V:B_V + 1])).astype(BF16)


def _mla_attn(q, k, v, batch, lp, s_real):
    tq = min(MLA_Q_TILE, s_real)
    tk = min(MLA_K_CHUNK, tq)
    nq = s_real // tq
    qmap = lambda b, h, i: (b, h, i, 0)
    kvmap = lambda b, h, i: (b, h, 0, 0)
    return pl.pallas_call(
        functools.partial(_mla_attn_kernel, s_real=s_real, tq=tq, tk=tk),
        grid=(batch, B_HEADS, nq + 1),
        in_specs=[
            pl.BlockSpec((1, 1, tq, B_HEAD_PAD), qmap),
            pl.BlockSpec((1, 1, lp, B_HEAD_PAD), kvmap),
            pl.BlockSpec((1, 1, lp, B_HEAD_PAD), kvmap),
        ],
        out_specs=pl.BlockSpec((1, tq, B_HEAD_PAD), lambda b, h, i: (b, i, h)),
        out_shape=jax.ShapeDtypeStruct((batch, lp, B_HEADS * B_HEAD_PAD), BF16),
        scratch_shapes=[pltpu.VMEM((tq, 128), F32), pltpu.VMEM((tq, B_HEAD_PAD), F32)],
        compiler_params=_params("parallel", "parallel", "arbitrary"),
        name="mla_attn",
    )(q, k, v)


def _positions(lp, s_real):
    r = jnp.arange(lp, dtype=jnp.int32)
    return jnp.where(r < s_real, r + N_META, r - s_real).astype(F32)


def _rope_angles(pos, dim):
    inv_freq = 1.0 / (ROPE_THETA ** (jnp.arange(0, dim, 2, dtype=F32) / dim))
    ang = pos[:, None] * inv_freq[None, :]
    return jnp.cos(ang), jnp.sin(ang)


def _swa_tables(pos):
    cos, sin = _rope_angles(pos, A_HEAD_DIM)
    cos_t = jnp.tile(cos, (1, 4))
    sin_t = jnp.tile(jnp.concatenate([-sin, sin], axis=1), (1, 2))
    return cos_t, sin_t


def _mla_tables(pos, scale):
    cos, sin = _rope_angles(pos, B_ROPE)
    n = pos.shape[0]
    tail = jnp.zeros((n, 128 - B_NOPE - B_ROPE), F32)
    ones = jnp.ones((n, B_NOPE), F32)
    zeros = jnp.zeros((n, B_NOPE), F32)
    cos_t = jnp.concatenate([ones, cos, cos, tail], axis=1) * scale
    sin_t = jnp.concatenate([zeros, -sin, sin, tail], axis=1) * scale
    return cos_t, sin_t


def _pad_heads(w, n_heads, width, pad_to):
    k = w.shape[0]
    w = w.reshape(k, n_heads, width)
    w = jnp.pad(w, ((0, 0), (0, 0), (0, pad_to - width)))
    return w.reshape(k, n_heads * pad_to)


def kernel(x, meta_tokens, attn_norm, ffn_norm, final_norm, a_w_qkv, a_b_qkv, a_sinks, a_w_o, a_b_o,
           b_w_in, b_q_norm, b_kv_norm, b_w_uq, b_w_ukv, b_w_o,
           moe_w_group, moe_b_group, moe_w_router, moe_b_router, moe_w_gate, moe_w_up, moe_w_down):
    batch, s_real, _ = x.shape
    lp = s_real + BLOCK
    tp = batch * lp
    assert tp % ROW_TILE == 0 and lp % ROW_TILE == 0 and s_real % BLOCK == 0

    meta = jnp.broadcast_to(meta_tokens[None].astype(x.dtype), (batch, N_META, D_MODEL))
    pad = jnp.zeros((batch, BLOCK - N_META, D_MODEL), x.dtype)
    h = jnp.concatenate([x, meta, pad], axis=1).reshape(tp, D_MODEL)
    pos = _positions(lp, s_real)

    def router_params(i):
        wr = jnp.concatenate([moe_w_router[i], moe_w_group[i]], axis=1)
        wr = jnp.pad(wr, ((0, 0), (0, ROUTE_LANES - wr.shape[1])))
        br = jnp.concatenate([moe_b_router[i], moe_b_group[i]])
        br = jnp.pad(br, (0, ROUTE_LANES - br.shape[0]))[None]
        w_hi = wr.astype(BF16)
        w_lo = (wr - w_hi.astype(F32)).astype(BF16)
        return jnp.concatenate([w_hi, w_lo], axis=1), br

    w_gate = moe_w_gate.reshape(-1, D_MODEL, D_EXPERT)
    w_up = moe_w_up.reshape(-1, D_MODEL, D_EXPERT)
    w_down = moe_w_down.reshape(-1, D_EXPERT, D_MODEL)

    def moe(i, h1, route_t, final):
        return _moe_layer(h1, route_t, ffn_norm[i][None], w_gate, w_up, w_down, i,
                          final_norm[None], batch, lp, s_real, final)

    wq, wk, wv = a_w_qkv[0][:, :A_Q_W], a_w_qkv[0][:, A_Q_W:A_Q_W + A_KV_W], a_w_qkv[0][:, A_Q_W + A_KV_W:]
    bq, bk, bv = a_b_qkv[0][:A_Q_W], a_b_qkv[0][A_Q_W:A_Q_W + A_KV_W], a_b_qkv[0][A_Q_W + A_KV_W:]
    swap = lambda t: jnp.concatenate([t[..., A_HEAD_DIM:], t[..., :A_HEAD_DIM]], axis=-1)
    w_a = jnp.concatenate([wq, wk, swap(wk), wv, swap(wv)], axis=1).astype(BF16)
    b_a = jnp.concatenate([bq, bk, swap(bk), bv, swap(bv)])[None]
    cos_a, sin_a = _swa_tables(pos)
    q, kk, vv = _swa_qkv(h, attn_norm[0][None], w_a, b_a, cos_a, sin_a, lp)
    sink_rows = jnp.repeat(a_sinks[0].astype(F32).reshape(A_KV_HEADS, 4, 2).transpose(0, 2, 1).reshape(4, 4), BLOCK, axis=1)[:, None, :]
    o = _swa_attn(q, kk, vv, sink_rows, batch, lp, s_real)
    wr, br = router_params(0)
    h1, route_t = _attn_out(o, a_w_o[0].astype(BF16), a_b_o[0][None], h, ffn_norm[0][None], wr, br)
    h = moe(0, h1, route_t, False).reshape(tp, D_MODEL)

    scale = (B_NOPE + B_ROPE) ** -0.5
    w_in = b_w_in[0]
    half = B_ROPE // 2
    w_pe = w_in[:, B_Q_RANK + B_KV_RANK:]
    kpe_cols = jnp.pad(jnp.concatenate([w_pe, w_pe[:, :half]], axis=1), ((0, 0), (B_NOPE, 128 - B_NOPE - B_ROPE - half)))
    w_in_p = jnp.concatenate([w_in[:, :B_Q_RANK + B_KV_RANK], kpe_cols], axis=1).astype(BF16)
    w_uq = b_w_uq[0].reshape(B_Q_RANK, B_HEADS, B_NOPE + B_ROPE)
    w_uq = jnp.concatenate([w_uq, w_uq[:, :, B_NOPE:B_NOPE + half]], axis=2).reshape(B_Q_RANK, -1)
    w_q_p = _pad_heads(w_uq, B_HEADS, B_NOPE + B_ROPE + half, B_HEAD_PAD).astype(BF16)
    w_ukv = b_w_ukv[0].reshape(B_KV_RANK, B_HEADS, B_NOPE + B_V)
    w_k_p = _pad_heads(w_ukv[:, :, :B_NOPE].reshape(B_KV_RANK, -1), B_HEADS, B_NOPE, B_HEAD_PAD).astype(BF16)
    w_v_p = _pad_heads(w_ukv[:, :, B_NOPE:].reshape(B_KV_RANK, -1), B_HEADS, B_V, B_HEAD_PAD).astype(BF16)
    tabs = _mla_tables(pos, scale * math.log2(math.e)) + _mla_tables(pos, 1.0)
    q, k, v = _mla_proj(h, attn_norm[1][None], w_in_p, b_q_norm[0][None], b_kv_norm[0][None],
                        w_q_p, w_k_p, w_v_p, tabs, batch, lp)
    o = _mla_attn(q, k, v, batch, lp, s_real).reshape(tp, B_HEADS * B_HEAD_PAD)
    w_o_p = jnp.pad(b_w_o[0].reshape(B_HEADS, B_V, D_MODEL), ((0, 0), (0, B_HEAD_PAD - B_V), (0, 0)))
    w_o_p = w_o_p.reshape(B_HEADS * B_HEAD_PAD, D_MODEL).astype(BF16)
    wr, br = router_params(1)
    h1, route_t = _attn_out(o, w_o_p, jnp.zeros((1, D_MODEL), F32), h, ffn_norm[1][None], wr, br)
    return moe(1, h1, route_t, True)
```

```python
import functools
import math

import jax
import jax.numpy as jnp
from jax import lax
from jax.experimental import pallas as pl
from jax.experimental.pallas import tpu as pltpu

F32 = jnp.float32
BF16 = jnp.bfloat16

D_MODEL = 1024
N_META = 16
BLOCK = 128
ROPE_THETA = 10000.0
NORM_EPS = 1e-6
NEG_INF = -1e30

A_HEADS = 16
A_KV_HEADS = 2
A_HEAD_DIM = 64
A_Q_W = A_HEADS * A_HEAD_DIM
A_KV_W = A_KV_HEADS * A_HEAD_DIM

B_HEADS = 16
B_NOPE = 64
B_ROPE = 32
B_V = 64
B_Q_RANK = 256
B_KV_RANK = 128
B_HEAD_PAD = 128

N_GROUPS = 4
EXPERTS_PER_GROUP = 8
N_EXPERTS = N_GROUPS * EXPERTS_PER_GROUP
TOP_K = 2
D_EXPERT = 256
MOE_BLOCK = 128
ROUTE_LANES = 128

ROW_TILE = 640
DMA_UNROLL = 8
MLA_Q_TILE = 2048
MLA_K_CHUNK = 1024
MLA_SUB_ROWS = 256
VMEM_LIMIT = 56 * 1024 * 1024


def _rms(x, g):
    return x * lax.rsqrt(jnp.mean(x * x, axis=-1, keepdims=True) + NORM_EPS) * g


def _params(*sem):
    return pltpu.CompilerParams(dimension_semantics=sem, vmem_limit_bytes=VMEM_LIMIT)


def _swa_qkv_kernel(h_ref, g_ref, w_ref, b_ref, cos_ref, sin_ref, q_ref, k_ref, v_ref):
    xn = _rms(h_ref[...], g_ref[...]).astype(BF16)
    y = jnp.dot(xn, w_ref[...], preferred_element_type=F32) + b_ref[...]
    cos = cos_ref[...]
    sin = sin_ref[...]
    lane = lax.broadcasted_iota(jnp.int32, cos.shape, 1)
    first = (lane & (A_HEAD_DIM - 1)) < (A_HEAD_DIM // 2)

    def rope(c):
        rot = jnp.where(first, pltpu.roll(c, 128 - A_HEAD_DIM // 2, 1), pltpu.roll(c, A_HEAD_DIM // 2, 1))
        return c * cos + rot * sin

    for j in range(A_Q_W // 128):
        q_ref[:, j * 128:(j + 1) * 128] = (rope(y[:, j * 128:(j + 1) * 128]) * (A_HEAD_DIM ** -0.5)).astype(BF16)
    k_ref[:, 0:128] = rope(y[:, A_Q_W:A_Q_W + 128]).astype(BF16)
    k_ref[:, 128:256] = rope(y[:, A_Q_W + 128:A_Q_W + 256]).astype(BF16)
    v_ref[...] = y[:, A_Q_W + 256:A_Q_W + 512].astype(BF16)


def _swa_qkv(h, g, w, b, cos, sin, lp):
    tp = h.shape[0]
    tiles_per_batch = lp // ROW_TILE
    n_out = w.shape[1]
    row = lambda i: (i, 0)
    fixed = lambda i: (0, 0)
    tab = lambda i: (i % tiles_per_batch, 0)
    return pl.pallas_call(
        _swa_qkv_kernel,
        grid=(tp // ROW_TILE,),
        in_specs=[
            pl.BlockSpec((ROW_TILE, D_MODEL), row),
            pl.BlockSpec((1, D_MODEL), fixed),
            pl.BlockSpec((D_MODEL, n_out), fixed),
            pl.BlockSpec((1, n_out), fixed),
            pl.BlockSpec((ROW_TILE, 128), tab),
            pl.BlockSpec((ROW_TILE, 128), tab),
        ],
        out_specs=[
            pl.BlockSpec((ROW_TILE, A_Q_W), row),
            pl.BlockSpec((ROW_TILE, 256), row),
            pl.BlockSpec((ROW_TILE, 256), row),
        ],
        out_shape=[
            jax.ShapeDtypeStruct((tp, A_Q_W), BF16),
            jax.ShapeDtypeStruct((tp, 256), BF16),
            jax.ShapeDtypeStruct((tp, 256), BF16),
        ],
        compiler_params=_params("parallel"),
        name="swa_qkv",
    )(h, g, w, b, cos, sin)


def _swa_attn_kernel(q_ref, kc_ref, kp_ref, km_ref, vc_ref, vp_ref, vm_ref, sink_ref, o_ref, *, n_real_blocks):
    n = pl.program_id(1)
    is_real = n < n_real_blocks
    n_keys = N_META + 2 * BLOCK
    big = jnp.int32(1 << 20)
    key = lax.broadcasted_iota(jnp.int32, (n_keys, BLOCK), 0)
    qry = lax.broadcasted_iota(jnp.int32, (n_keys, BLOCK), 1)
    jp = key - N_META
    jc = key - (N_META + BLOCK)
    meta_lim = jnp.where(is_real, N_META, 0)
    prev_off = jnp.where(jnp.logical_and(is_real, n >= 1), 0, big)
    cur_lim = jnp.where(is_real, big, N_META)
    ok = (key < meta_lim) | ((jc < 0) & (jp > qry + prev_off)) | ((jc >= 0) & (jc <= qry) & (jc < cur_lim))
    bias1 = jnp.where(ok, 0.0, NEG_INF).astype(F32)
    bias = jnp.concatenate([bias1] * 4, axis=1)

    kcat = jnp.concatenate([km_ref[0], kp_ref[0], kc_ref[0]], axis=0)
    vcat = jnp.concatenate([vm_ref[0], vp_ref[0], vc_ref[0]], axis=0)
    lo = lax.broadcasted_iota(jnp.int32, (n_keys, 128), 1) < A_HEAD_DIM
    zero = jnp.zeros((n_keys, 128), BF16)

    def scores(kvh, parity):
        a, b = (kcat[:, :128], kcat[:, 128:]) if kvh == 0 else (kcat[:, 128:], kcat[:, :128])
        kx = jnp.where(lo, a, zero) if parity == 0 else jnp.where(lo, zero, b)
        base = kvh * (A_Q_W // A_KV_HEADS)
        qs = jnp.concatenate([q_ref[0, :, base + c * 128:base + (c + 1) * 128] for c in range(4)], axis=0)
        st = lax.dot_general(kx, qs, (((1,), (1,)), ((), ())), preferred_element_type=F32) + bias

        def finish():
            a, b = (vcat[:, :128], vcat[:, 128:]) if kvh == 0 else (vcat[:, 128:], vcat[:, :128])
            vx = jnp.where(lo, a, zero) if parity == 0 else jnp.where(lo, zero, b)
            sink = sink_ref[kvh * 2 + parity]
            m = jnp.maximum(jnp.max(st, axis=0, keepdims=True), sink)
            p = jnp.exp(st - m)
            denom = jnp.sum(p, axis=0, keepdims=True) + jnp.exp(sink - m)
            ot = lax.dot_general(vx, p.astype(BF16), (((0,), (0,)), ((), ())), preferred_element_type=F32)
            return ot * (1.0 / denom)
        return finish

    for kvh in range(A_KV_HEADS):
        base = kvh * (A_Q_W // A_KV_HEADS)
        even = scores(kvh, 0)
        odd = scores(kvh, 1)
        ot = even() + odd()
        for c in range(4):
            o_ref[0, :, base + c * 128:base + (c + 1) * 128] = ot[:, c * BLOCK:(c + 1) * BLOCK].T.astype(BF16)


def _swa_attn(q, kk, vv, sink_rows, batch, lp, s_real):
    nb = s_real // BLOCK
    q3 = q.reshape(batch, lp, A_Q_W)
    k3 = kk.reshape(batch, lp, 256)
    v3 = vv.reshape(batch, lp, 256)
    cur = lambda b, n: (b, n, 0)
    prev = lambda b, n: (b, jnp.maximum(n - 1, 0), 0)
    meta = lambda b, n: (b, s_real // N_META, 0)
    out = pl.pallas_call(
        functools.partial(_swa_attn_kernel, n_real_blocks=nb),
        grid=(batch, nb + 1),
        in_specs=[
            pl.BlockSpec((1, BLOCK, A_Q_W), cur),
            pl.BlockSpec((1, BLOCK, 256), cur),
            pl.BlockSpec((1, BLOCK, 256), prev),
            pl.BlockSpec((1, N_META, 256), meta),
            pl.BlockSpec((1, BLOCK, 256), cur),
            pl.BlockSpec((1, BLOCK, 256), prev),
            pl.BlockSpec((1, N_META, 256), meta),
            pl.BlockSpec((4, 1, 4 * BLOCK), lambda b, n: (0, 0, 0)),
        ],
        out_specs=pl.BlockSpec((1, BLOCK, A_Q_W), cur),
        out_shape=jax.ShapeDtypeStruct((batch, lp, A_Q_W), BF16),
        compiler_params=_params("parallel", "parallel"),
        name="swa_attn",
    )(q3, k3, k3, k3, v3, v3, v3, sink_rows)
    return out.reshape(batch * lp, A_Q_W)


ROUTE_ROWS = 40


def _route(xn, wr_ref, br):
    x_hi = xn.astype(BF16)
    x_lo = (xn - x_hi.astype(F32)).astype(BF16)
    both = jnp.dot(x_hi, wr_ref[...], preferred_element_type=F32)
    lg = (both[:, :ROUTE_LANES] + both[:, ROUTE_LANES:]
          + jnp.dot(x_lo, wr_ref[:, :ROUTE_LANES], preferred_element_type=F32) + br)
    lt = lg.T[0:ROUTE_ROWS]
    row_i = lax.broadcasted_iota(jnp.int32, lt.shape, 0)
    row = row_i.astype(F32)
    row_grp = (row_i >> 3).astype(F32)
    big = 1e9
    is_g = jnp.logical_and(row_i >= N_EXPERTS, row_i < N_EXPERTS + N_GROUPS)
    gl = jnp.where(is_g, lt, -jnp.inf)
    gmax = jnp.max(gl, axis=0, keepdims=True)
    g_p = 1.0 / jnp.sum(jnp.exp(gl - gmax), axis=0, keepdims=True)
    g_idx = jnp.min(jnp.where(gl == gmax, row - float(N_EXPERTS), big), axis=0, keepdims=True)
    el = jnp.where(row_grp == g_idx, lt, -jnp.inf)
    m1 = jnp.max(el, axis=0, keepdims=True)
    esum = jnp.sum(jnp.exp(el - m1), axis=0, keepdims=True)
    i1 = jnp.min(jnp.where(el == m1, row, big), axis=0, keepdims=True)
    el2 = jnp.where(row == i1, -jnp.inf, el)
    m2 = jnp.max(el2, axis=0, keepdims=True)
    i2 = jnp.min(jnp.where(el2 == m2, row, big), axis=0, keepdims=True)
    e1 = 1.0 / esum
    e2 = jnp.exp(m2 - m1) / esum
    w1 = g_p * e1 / (e1 + e2)
    w2 = g_p * e2 / (e1 + e2)
    return jnp.concatenate([i1, i2, w1, w2, jnp.zeros((4, lt.shape[1]), F32)], axis=0)


def _attn_out_kernel(o_ref, w_ref, b_ref, h_ref, g_ref, wr_ref, br_ref, h1_ref, route_t_ref):
    h1 = h_ref[...] + jnp.dot(o_ref[...], w_ref[...], preferred_element_type=F32) + b_ref[...]
    h1_ref[...] = h1
    route_t_ref[...] = _route(_rms(h1, g_ref[...]), wr_ref, br_ref[...])


def _attn_out(o, w, b, h, g, wr, br):
    tp, ko = o.shape
    row = lambda i: (i, 0)
    fixed = lambda i: (0, 0)
    return pl.pallas_call(
        _attn_out_kernel,
        grid=(tp // ROW_TILE,),
        in_specs=[
            pl.BlockSpec((ROW_TILE, ko), row),
            pl.BlockSpec((ko, D_MODEL), fixed),
            pl.BlockSpec((1, D_MODEL), fixed),
            pl.BlockSpec((ROW_TILE, D_MODEL), row),
            pl.BlockSpec((1, D_MODEL), fixed),
            pl.BlockSpec((D_MODEL, 2 * ROUTE_LANES), fixed),
            pl.BlockSpec((1, ROUTE_LANES), fixed),
        ],
        out_specs=[
            pl.BlockSpec((ROW_TILE, D_MODEL), row),
            pl.BlockSpec((8, ROW_TILE), lambda i: (0, i)),
        ],
        out_shape=[
            jax.ShapeDtypeStruct((tp, D_MODEL), F32),
            jax.ShapeDtypeStruct((8, tp), F32),
        ],
        compiler_params=_params("parallel"),
        name="attn_out_route",
    )(o, w, b, h, g, wr, br)


def _plan_kernel(rt_ref, pos_ref, be_ref, pends_ref, cnt_col, carry, *, n_blocks_pad):
    phase = pl.program_id(0)
    i = pl.program_id(1)
    tm = rt_ref.shape[1]
    id0 = rt_ref[0:1, :]
    id1 = rt_ref[1:2, :]
    e_sub = lax.broadcasted_iota(jnp.int32, (N_EXPERTS, tm), 0).astype(F32)
    hit0 = e_sub == id0
    hit1 = e_sub == id1
    member_t = jnp.where(hit0, 1.0, jnp.where(hit1, 1.0, 0.0))
    tile_cnt_col = jnp.sum(member_t, axis=1, keepdims=True)

    @pl.when(jnp.logical_and(phase == 0, i == 0))
    def _():
        cnt_col[...] = jnp.zeros(cnt_col.shape, F32)

    @pl.when(phase == 0)
    def _():
        cnt_col[...] += jnp.broadcast_to(tile_cnt_col, cnt_col.shape)

    @pl.when(jnp.logical_and(phase == 1, i == 0))
    def _():
        pad_to_block = lambda c: jnp.floor((c + (MOE_BLOCK - 1.0)) * (1.0 / MOE_BLOCK)) * MOE_BLOCK
        padded_col = pad_to_block(cnt_col[...])
        square = jnp.concatenate([padded_col, jnp.zeros((128 - N_EXPERTS, 128), F32)], axis=0)
        padded_row = square.T[0:1]
        lane_e = lax.broadcasted_iota(jnp.int32, (N_EXPERTS, 128), 1)
        sub_e = lax.broadcasted_iota(jnp.int32, (N_EXPERTS, 128), 0)
        pstart_col = jnp.sum(jnp.where(lane_e < sub_e, padded_row, 0.0), axis=1, keepdims=True)
        pends_row = jnp.sum(jnp.where(sub_e <= lane_e, padded_col, 0.0), axis=0, keepdims=True)
        pends_col = pstart_col + padded_col[:, 0:1]
        carry[...] = jnp.broadcast_to(pstart_col, carry.shape)
        pends_ref[...] = pends_row.astype(jnp.int32)
        blk_lane = lax.broadcasted_iota(jnp.int32, (N_EXPERTS, n_blocks_pad), 1)
        blk_start = (blk_lane * MOE_BLOCK).astype(F32)
        be = jnp.sum(jnp.where(pends_col <= blk_start, 1.0, 0.0), axis=0, keepdims=True)
        be = jnp.minimum(be, N_EXPERTS - 1.0)
        n_used = pends_row[:, N_EXPERTS - 1:N_EXPERTS] * (1.0 / MOE_BLOCK)
        be = jnp.where(blk_lane[0:1] == n_blocks_pad - 1, n_used, be)
        be_ref[...] = be.astype(jnp.int32)

    @pl.when(phase == 1)
    def _():
        before = lax.broadcasted_iota(jnp.int32, (tm, tm), 0) < lax.broadcasted_iota(jnp.int32, (tm, tm), 1)
        prefix = jnp.dot(member_t.astype(BF16), jnp.where(before, 1.0, 0.0).astype(BF16),
                         preferred_element_type=F32)
        row_of = prefix + carry[:, 0:1]
        dest0 = jnp.sum(jnp.where(hit0, row_of, 0.0), axis=0, keepdims=True)
        dest1 = jnp.sum(jnp.where(hit1, row_of, 0.0), axis=0, keepdims=True)
        pos_ref[...] = jnp.concatenate([dest0, dest1], axis=0).astype(jnp.int32)
        carry[...] += jnp.broadcast_to(tile_cnt_col, carry.shape)


def _dispatch_plan(route_t, n_blocks):
    tp = route_t.shape[1]
    n_tiles = tp // ROW_TILE
    n_blocks_pad = -(-(n_blocks + 1) // 128) * 128
    pos, be, pends = pl.pallas_call(
        functools.partial(_plan_kernel, n_blocks_pad=n_blocks_pad),
        grid=(2, n_tiles),
        in_specs=[
            pl.BlockSpec((8, ROW_TILE), lambda p, i: (0, i)),
        ],
        out_specs=[
            pl.BlockSpec((TOP_K, ROW_TILE), lambda p, i: (0, i * p)),
            pl.BlockSpec((1, n_blocks_pad), lambda p, i: (0, 0)),
            pl.BlockSpec((1, 128), lambda p, i: (0, 0)),
        ],
        out_shape=[
            jax.ShapeDtypeStruct((TOP_K, tp), jnp.int32),
            jax.ShapeDtypeStruct((1, n_blocks_pad), jnp.int32),
            jax.ShapeDtypeStruct((1, 128), jnp.int32),
        ],
        scratch_shapes=[pltpu.VMEM((N_EXPERTS, 128), F32), pltpu.VMEM((N_EXPERTS, 128), F32)],
        compiler_params=_params("arbitrary", "arbitrary"),
        name="moe_plan",
    )(route_t)
    return pos.reshape(-1), be.reshape(-1), pends.reshape(-1)


def _scatter_row(stage, slot, r, buf_hbm, dst_row, sem):
    return pltpu.make_async_copy(stage.at[slot, pl.ds(r, 1)], buf_hbm.at[pl.ds(dst_row, 1)], sem.at[slot])


def _dispatch_kernel(pos_ref, pends_ref, h_ref, buf_hbm, stage, zsem, sem, *, n_tokens, n_blocks):
    i = pl.program_id(0)
    slot = i % 2

    def zero_block(row0):
        return pltpu.make_async_copy(stage.at[0], buf_hbm.at[pl.ds(pl.multiple_of(row0, MOE_BLOCK), MOE_BLOCK)], zsem)

    def zero_tail(e):
        return zero_block(jnp.maximum(pends_ref[e] - MOE_BLOCK, 0))

    @pl.when(i == 0)
    def _():
        stage[0] = jnp.zeros((BLOCK, D_MODEL), F32)
        for e in range(N_EXPERTS):
            zero_tail(e).start()
        for e in range(N_EXPERTS):
            zero_tail(e).wait()
        first_unused = pends_ref[N_EXPERTS - 1] // MOE_BLOCK

        def start_unused(b, carry):
            zero_block(b * MOE_BLOCK).start()
            return carry

        def wait_unused(b, carry):
            zero_block(b * MOE_BLOCK).wait()
            return carry
        lax.fori_loop(first_unused, n_blocks, start_unused, 0)
        lax.fori_loop(first_unused, n_blocks, wait_unused, 0)

    stage[slot] = h_ref[...]

    def issue(c, carry):
        for u in range(DMA_UNROLL):
            r = pl.multiple_of(c * DMA_UNROLL, DMA_UNROLL) + u
            t = i * BLOCK + r
            _scatter_row(stage, slot, r, buf_hbm, pos_ref[t], sem).start(priority=0)
            _scatter_row(stage, slot, r, buf_hbm, pos_ref[n_tokens + t], sem).start(priority=1)
        return carry
    lax.fori_loop(0, BLOCK // DMA_UNROLL, issue, 0)

    def drain(which):
        for _ in range(2 * BLOCK):
            _scatter_row(stage, which, 0, buf_hbm, 0, sem).wait()

    @pl.when(i > 0)
    def _():
        drain(1 - slot)

    @pl.when(i == pl.num_programs(0) - 1)
    def _():
        drain(slot)


def _moe_dispatch(h1, pos, pends, n_rows):
    tp = h1.shape[0]
    grid_spec = pltpu.PrefetchScalarGridSpec(
        num_scalar_prefetch=2,
        grid=(tp // BLOCK,),
        in_specs=[pl.BlockSpec((BLOCK, D_MODEL), lambda i, pos, pends: (i, 0))],
        out_specs=pl.BlockSpec(memory_space=pl.ANY),
        scratch_shapes=[
            pltpu.VMEM((2, BLOCK, D_MODEL), F32),
            pltpu.SemaphoreType.DMA(()),
            pltpu.SemaphoreType.DMA((2,)),
        ],
    )
    return pl.pallas_call(
        functools.partial(_dispatch_kernel, n_tokens=tp, n_blocks=n_rows // MOE_BLOCK),
        grid_spec=grid_spec,
        out_shape=jax.ShapeDtypeStruct((n_rows, D_MODEL), F32),
        compiler_params=_params("arbitrary"),
        name="moe_dispatch",
    )(pos, pends, h1)


def _moe_kernel(be_ref, x_ref, g_ref, wga_ref, wua_ref, wda_ref, wgb_ref, wub_ref, wdb_ref, y_ref,
                wg_bf, wu_bf, wd_bf, *, n_blocks_pad):
    i = pl.program_id(0)
    n_used = be_ref[n_blocks_pad - 1]
    weights = ((wga_ref, wua_ref, wda_ref), (wgb_ref, wub_ref, wdb_ref))

    def expert_of(blk):
        return be_ref[jnp.clip(blk, 0, n_used - 1)]

    for half, (wg_ref, wu_ref, wd_ref) in enumerate(weights):
        blk = 2 * i + half

        @pl.when(jnp.logical_or(i == 0, expert_of(blk) != expert_of(blk - 2)))
        def _():
            wg_bf[half] = wg_ref[0].astype(BF16)
            wu_bf[half] = wu_ref[0].astype(BF16)
            wd_bf[half] = wd_ref[0].astype(BF16)

    def rows(half):
        return slice(half * MOE_BLOCK, (half + 1) * MOE_BLOCK)

    def gate_up(half):
        xn = _rms(x_ref[rows(half), :], g_ref[...]).astype(BF16)
        return (jnp.dot(xn, wg_bf[half], preferred_element_type=F32),
                jnp.dot(xn, wu_bf[half], preferred_element_type=F32))

    def down(half, gate, up):
        act = (gate * jax.nn.sigmoid(gate) * up).astype(BF16)
        y_ref[rows(half), :] = jnp.dot(act, wd_bf[half], preferred_element_type=F32)

    def zero(half):
        y_ref[rows(half), :] = jnp.zeros((MOE_BLOCK, D_MODEL), F32)

    used_a = 2 * i < n_used
    used_b = 2 * i + 1 < n_used

    @pl.when(used_b)
    def _():
        ga, ua = gate_up(0)
        gb, ub = gate_up(1)
        down(0, ga, ua)
        down(1, gb, ub)

    @pl.when(jnp.logical_and(used_a, jnp.logical_not(used_b)))
    def _():
        down(0, *gate_up(0))
        zero(1)

    @pl.when(jnp.logical_not(used_a))
    def _():
        zero(0)
        zero(1)


def _moe_experts(buf, g, wg, wu, wd, layer, block_expert, n_blocks):
    n_blocks_pad = block_expert.shape[0]
    assert n_blocks % 2 == 0
    last = lambda be: be[n_blocks_pad - 1] - 1
    pair = lambda i, be: (jnp.minimum(i, last(be) // 2), 0)
    ew = lambda half: (lambda i, be: (layer * N_EXPERTS + be[jnp.minimum(2 * i + half, last(be))], 0, 0))
    weights = lambda half: [
        pl.BlockSpec((1, D_MODEL, D_EXPERT), ew(half)),
        pl.BlockSpec((1, D_MODEL, D_EXPERT), ew(half)),
        pl.BlockSpec((1, D_EXPERT, D_MODEL), ew(half)),
    ]
    grid_spec = pltpu.PrefetchScalarGridSpec(
        num_scalar_prefetch=1,
        grid=(n_blocks // 2,),
        in_specs=[
            pl.BlockSpec((2 * MOE_BLOCK, D_MODEL), pair),
            pl.BlockSpec((1, D_MODEL), lambda i, be: (0, 0)),
        ] + weights(0) + weights(1),
        out_specs=pl.BlockSpec((2 * MOE_BLOCK, D_MODEL), lambda i, be: (i, 0)),
        scratch_shapes=[
            pltpu.VMEM((2, D_MODEL, D_EXPERT), BF16),
            pltpu.VMEM((2, D_MODEL, D_EXPERT), BF16),
            pltpu.VMEM((2, D_EXPERT, D_MODEL), BF16),
        ],
    )
    return pl.pallas_call(
        functools.partial(_moe_kernel, n_blocks_pad=n_blocks_pad),
        grid_spec=grid_spec,
        out_shape=jax.ShapeDtypeStruct((n_blocks * MOE_BLOCK, D_MODEL), F32),
        compiler_params=_params("arbitrary"),
        name="moe_experts",
    )(block_expert, buf, g, wg, wu, wd, wg, wu, wd)


def _row_copy(src_hbm, src_row, dst, slot, dst_row, sem):
    return pltpu.make_async_copy(src_hbm.at[pl.ds(src_row, 1)], dst.at[slot, pl.ds(dst_row, 1)], sem.at[slot])


def _combine_kernel(pos_ref, h_ref, rt_ref, g_ref, y_hbm, o_ref, buf, sem, *, tiles_per_batch, n_tokens, final):
    b = pl.program_id(0)
    j = pl.program_id(1)
    nj = pl.num_programs(1)
    step = b * nj + j
    slot = step % 2

    def issue(bb, jj, to_slot):
        tile = bb * tiles_per_batch + jj
        def body(c, carry):
            for u in range(DMA_UNROLL):
                r = pl.multiple_of(c * DMA_UNROLL, DMA_UNROLL) + u
                t = tile * BLOCK + r
                _row_copy(y_hbm, pos_ref[t], buf, to_slot, r, sem).start(priority=0)
                _row_copy(y_hbm, pos_ref[n_tokens + t], buf, to_slot, BLOCK + r, sem).start(priority=1)
            return carry
        lax.fori_loop(0, BLOCK // DMA_UNROLL, body, 0)

    @pl.when(step == 0)
    def _():
        issue(b, j, 0)

    @pl.when(step + 1 < pl.num_programs(0) * nj)
    def _():
        nxt = j + 1
        wrap = nxt == nj
        issue(jnp.where(wrap, b + 1, b), jnp.where(wrap, 0, nxt), 1 - slot)

    for _ in range(2 * BLOCK):
        _row_copy(y_hbm, 0, buf, slot, 0, sem).wait()

    w = jnp.concatenate([rt_ref[...], jnp.zeros((BLOCK - 8, BLOCK), F32)], axis=0).T
    h2 = h_ref[0] + w[:, TOP_K:TOP_K + 1] * buf[slot, 0:BLOCK] + w[:, TOP_K + 1:TOP_K + 2] * buf[slot, BLOCK:2 * BLOCK]
    if final:
        h2 = _rms(h2, g_ref[...])
    o_ref[0] = h2


def _moe_combine(h1, route_t, y, pos, g, batch, lp, s_real, final):
    tiles_per_batch = lp // BLOCK
    nj = s_real // BLOCK if final else tiles_per_batch
    out_rows = s_real if final else lp
    tile = lambda b, j, pos: (b, j, 0)
    grid_spec = pltpu.PrefetchScalarGridSpec(
        num_scalar_prefetch=1,
        grid=(batch, nj),
        in_specs=[
            pl.BlockSpec((1, BLOCK, D_MODEL), tile),
            pl.BlockSpec((8, BLOCK), lambda b, j, pos: (0, b * tiles_per_batch + j)),
            pl.BlockSpec((1, D_MODEL), lambda b, j, pos: (0, 0)),
            pl.BlockSpec(memory_space=pl.ANY),
        ],
        out_specs=pl.BlockSpec((1, BLOCK, D_MODEL), tile),
        scratch_shapes=[
            pltpu.VMEM((2, 2 * BLOCK, D_MODEL), F32),
            pltpu.SemaphoreType.DMA((2,)),
        ],
    )
    return pl.pallas_call(
        functools.partial(_combine_kernel, tiles_per_batch=tiles_per_batch, n_tokens=batch * lp, final=final),
        grid_spec=grid_spec,
        out_shape=jax.ShapeDtypeStruct((batch, out_rows, D_MODEL), F32),
        compiler_params=_params("arbitrary", "arbitrary"),
        name="moe_combine_final" if final else "moe_combine",
    )(pos, h1.reshape(batch, lp, D_MODEL), route_t, g, y)


def _moe_layer(h1, route_t, g, wg, wu, wd, layer, final_g, batch, lp, s_real, final):
    tp = batch * lp
    n_blocks = -(-(tp * TOP_K + N_EXPERTS * (MOE_BLOCK - 1)) // MOE_BLOCK)
    n_blocks += n_blocks % 2
    pos, block_expert, pends = _dispatch_plan(route_t, n_blocks)
    buf = _moe_dispatch(h1, pos, pends, n_blocks * MOE_BLOCK)
    y = _moe_experts(buf, g, wg, wu, wd, layer, block_expert, n_blocks)
    return _moe_combine(h1, route_t, y, pos, final_g, batch, lp, s_real, final)


def _mla_rope(c, cos, sin):
    return c * cos + pltpu.roll(c, 128 - B_ROPE // 2, 1) * sin


def _mla_proj_kernel(h_ref, g_ref, win_ref, qn_ref, kvn_ref, wq_ref, wk_ref, wv_ref,
                     qc_ref, qs_ref, kc_ref, ks_ref, q_ref, k_ref, v_ref):
    xn = _rms(h_ref[...], g_ref[...]).astype(BF16)
    c = jnp.dot(xn, win_ref[...], preferred_element_type=F32)
    cq = _rms(c[:, :B_Q_RANK], qn_ref[...]).astype(BF16)
    ckv = _rms(c[:, B_Q_RANK:B_Q_RANK + B_KV_RANK], kvn_ref[...]).astype(BF16)
    kpe = _mla_rope(c[:, B_Q_RANK + B_KV_RANK:], kc_ref[...], ks_ref[...])
    q = jnp.dot(cq, wq_ref[...], preferred_element_type=F32)
    k = jnp.dot(ckv, wk_ref[...], preferred_element_type=F32)
    v = jnp.dot(ckv, wv_ref[...], preferred_element_type=F32)
    qc, qs = qc_ref[...], qs_ref[...]
    ones_col = (lax.broadcasted_iota(jnp.int32, kpe.shape, 1) == B_V).astype(F32)
    for hd in range(B_HEADS):
        sl = slice(hd * B_HEAD_PAD, (hd + 1) * B_HEAD_PAD)
        q_ref[0, hd] = _mla_rope(q[:, sl], qc, qs).astype(BF16)
        k_ref[0, hd] = (k[:, sl] + kpe).astype(BF16)
        v_ref[0, hd] = (v[:, sl] + ones_col).astype(BF16)


def _mla_proj(h, g, win, qn, kvn, wq, wk, wv, tabs, batch, lp):
    tiles_per_batch = lp // ROW_TILE
    row = lambda b, i: (b * tiles_per_batch + i, 0)
    fixed = lambda b, i: (0, 0)
    tab = lambda b, i: (i, 0)
    head_out = pl.BlockSpec((1, B_HEADS, ROW_TILE, B_HEAD_PAD), lambda b, i: (b, 0, i, 0))
    hw = B_HEADS * B_HEAD_PAD
    shape = jax.ShapeDtypeStruct((batch, B_HEADS, lp, B_HEAD_PAD), BF16)
    return pl.pallas_call(
        _mla_proj_kernel,
        grid=(batch, tiles_per_batch),
        in_specs=[
            pl.BlockSpec((ROW_TILE, D_MODEL), row),
            pl.BlockSpec((1, D_MODEL), fixed),
            pl.BlockSpec((D_MODEL, 512), fixed),
            pl.BlockSpec((1, B_Q_RANK), fixed),
            pl.BlockSpec((1, B_KV_RANK), fixed),
            pl.BlockSpec((B_Q_RANK, hw), fixed),
            pl.BlockSpec((B_KV_RANK, hw), fixed),
            pl.BlockSpec((B_KV_RANK, hw), fixed),
        ] + [pl.BlockSpec((ROW_TILE, 128), tab)] * 4,
        out_specs=[head_out, head_out, head_out],
        out_shape=[shape, shape, shape],
        compiler_params=_params("parallel", "parallel"),
        name="mla_proj",
    )(h, g, win, qn, kvn, wq, wk, wv, *tabs)


def _mla_attn_kernel(q_ref, k_ref, v_ref, o_ref, m_sc, acc_sc, *, s_real, tq, tk):
    qi = pl.program_id(2)
    nq = s_real // tq
    is_meta_q = qi == nq
    nt = (((1,), (1,)), ((), ()))
    sub = min(MLA_SUB_ROWS, tq)
    n_sub = tq // sub

    m_sc[...] = jnp.full(m_sc.shape, NEG_INF, F32)
    acc_sc[...] = jnp.zeros(acc_sc.shape, F32)

    def item(r, key0, width, mask_fn, with_meta=False):
        rows = slice(r * sub, (r + 1) * sub)
        k = k_ref[0, 0, pl.ds(key0, width), :]
        if with_meta:
            k = jnp.concatenate([k_ref[0, 0, s_real:s_real + BLOCK, :], k], axis=0)
        s = lax.dot_general(q_ref[0, 0, rows, :], k, nt, preferred_element_type=F32)
        if mask_fn is not None:
            rr = lax.broadcasted_iota(jnp.int32, s.shape, 0) + r * sub
            cc = lax.broadcasted_iota(jnp.int32, s.shape, 1)
            s = jnp.where(mask_fn(rr, cc), s, NEG_INF)

        def finish():
            v = v_ref[0, 0, pl.ds(key0, width), :]
            if with_meta:
                v = jnp.concatenate([v_ref[0, 0, s_real:s_real + BLOCK, :], v], axis=0)
            m_old = m_sc[rows]
            m_new = jnp.maximum(m_old, jnp.max(s, axis=-1, keepdims=True))
            p = jnp.exp2(s - jnp.concatenate([m_new] * (s.shape[1] // 128), axis=1)).astype(BF16)
            acc_sc[rows] = acc_sc[rows] * jnp.exp2(m_old - m_new) + jnp.dot(p, v, preferred_element_type=F32)
            m_sc[rows] = m_new
        return finish

    def run(items):
        pending = None
        for make in items:
            nxt = make()
            if pending is not None:
                pending()
            pending = nxt
        pending()

    def full_body(j, carry):
        start = pl.multiple_of(j * tk, tk)
        run([functools.partial(item, r, start, tk, None) for r in range(n_sub)])
        return carry

    n_full = jnp.where(is_meta_q, 0, qi * (tq // tk))
    lax.fori_loop(0, n_full, full_body, 0)

    @pl.when(jnp.logical_not(is_meta_q))
    def _():
        base = pl.multiple_of(qi * tq, tq)
        diag_mask = lambda rr, cc: jnp.where(cc < BLOCK, cc, cc - BLOCK) <= jnp.where(cc < BLOCK, N_META - 1, rr)
        run([functools.partial(item, r, base, (r + 1) * sub, diag_mask, True) for r in range(n_sub)])
        acc = acc_sc[...]
        o_ref[0] = (acc * (1.0 / acc[:, B_V:B_V + 1])).astype(BF16)

    @pl.when(is_meta_q)
    def _():
        rows = min(sub, BLOCK)
        item(0, s_real, BLOCK, lambda rr, cc: (cc <= rr) & (cc < N_META))()
        acc = acc_sc[0:rows]
        o_ref[0, 0:rows] = (acc * (1.0 / acc[:, B_V:B_V + 1])).astype(BF16)


def _mla_attn(q, k, v, batch, lp, s_real):
    tq = min(MLA_Q_TILE, s_real)
    tk = min(MLA_K_CHUNK, tq)
    nq = s_real // tq
    qmap = lambda b, h, i: (b, h, i, 0)
    kvmap = lambda b, h, i: (b, h, 0, 0)
    return pl.pallas_call(
        functools.partial(_mla_attn_kernel, s_real=s_real, tq=tq, tk=tk),
        grid=(batch, B_HEADS, nq + 1),
        in_specs=[
            pl.BlockSpec((1, 1, tq, B_HEAD_PAD), qmap),
            pl.BlockSpec((1, 1, lp, B_HEAD_PAD), kvmap),
            pl.BlockSpec((1, 1, lp, B_HEAD_PAD), kvmap),
        ],
        out_specs=pl.BlockSpec((1, tq, B_HEAD_PAD), lambda b, h, i: (b, i, h)),
        out_shape=jax.ShapeDtypeStruct((batch, lp, B_HEADS * B_HEAD_PAD), BF16),
        scratch_shapes=[pltpu.VMEM((tq, 128), F32), pltpu.VMEM((tq, B_HEAD_PAD), F32)],
        compiler_params=_params("parallel", "parallel", "arbitrary"),
        name="mla_attn",
    )(q, k, v)


def _positions(lp, s_real):
    r = jnp.arange(lp, dtype=jnp.int32)
    return jnp.where(r < s_real, r + N_META, r - s_real).astype(F32)


def _rope_angles(pos, dim):
    inv_freq = 1.0 / (ROPE_THETA ** (jnp.arange(0, dim, 2, dtype=F32) / dim))
    ang = pos[:, None] * inv_freq[None, :]
    return jnp.cos(ang), jnp.sin(ang)


def _swa_tables(pos):
    cos, sin = _rope_angles(pos, A_HEAD_DIM)
    cos_t = jnp.tile(cos, (1, 4))
    sin_t = jnp.tile(jnp.concatenate([-sin, sin], axis=1), (1, 2))
    return cos_t, sin_t


def _mla_tables(pos, scale):
    cos, sin = _rope_angles(pos, B_ROPE)
    n = pos.shape[0]
    tail = jnp.zeros((n, 128 - B_NOPE - B_ROPE), F32)
    ones = jnp.ones((n, B_NOPE), F32)
    zeros = jnp.zeros((n, B_NOPE), F32)
    cos_t = jnp.concatenate([ones, cos, cos, tail], axis=1) * scale
    sin_t = jnp.concatenate([zeros, -sin, sin, tail], axis=1) * scale
    return cos_t, sin_t


def _pad_heads(w, n_heads, width, pad_to):
    k = w.shape[0]
    w = w.reshape(k, n_heads, width)
    w = jnp.pad(w, ((0, 0), (0, 0), (0, pad_to - width)))
    return w.reshape(k, n_heads * pad_to)


def kernel(x, meta_tokens, attn_norm, ffn_norm, final_norm, a_w_qkv, a_b_qkv, a_sinks, a_w_o, a_b_o,
           b_w_in, b_q_norm, b_kv_norm, b_w_uq, b_w_ukv, b_w_o,
           moe_w_group, moe_b_group, moe_w_router, moe_b_router, moe_w_gate, moe_w_up, moe_w_down):
    batch, s_real, _ = x.shape
    lp = s_real + BLOCK
    tp = batch * lp
    assert tp % ROW_TILE == 0 and lp % ROW_TILE == 0 and s_real % BLOCK == 0

    meta = jnp.broadcast_to(meta_tokens[None].astype(x.dtype), (batch, N_META, D_MODEL))
    pad = jnp.zeros((batch, BLOCK - N_META, D_MODEL), x.dtype)
    h = jnp.concatenate([x, meta, pad], axis=1).reshape(tp, D_MODEL)
    pos = _positions(lp, s_real)

    def router_params(i):
        wr = jnp.concatenate([moe_w_router[i], moe_w_group[i]], axis=1)
        wr = jnp.pad(wr, ((0, 0), (0, ROUTE_LANES - wr.shape[1])))
        br = jnp.concatenate([moe_b_router[i], moe_b_group[i]])
        br = jnp.pad(br, (0, ROUTE_LANES - br.shape[0]))[None]
        w_hi = wr.astype(BF16)
        w_lo = (wr - w_hi.astype(F32)).astype(BF16)
        return jnp.concatenate([w_hi, w_lo], axis=1), br

    w_gate = moe_w_gate.reshape(-1, D_MODEL, D_EXPERT)
    w_up = moe_w_up.reshape(-1, D_MODEL, D_EXPERT)
    w_down = moe_w_down.reshape(-1, D_EXPERT, D_MODEL)

    def moe(i, h1, route_t, final):
        return _moe_layer(h1, route_t, ffn_norm[i][None], w_gate, w_up, w_down, i,
                          final_norm[None], batch, lp, s_real, final)

    wq, wk, wv = a_w_qkv[0][:, :A_Q_W], a_w_qkv[0][:, A_Q_W:A_Q_W + A_KV_W], a_w_qkv[0][:, A_Q_W + A_KV_W:]
    bq, bk, bv = a_b_qkv[0][:A_Q_W], a_b_qkv[0][A_Q_W:A_Q_W + A_KV_W], a_b_qkv[0][A_Q_W + A_KV_W:]
    swap = lambda t: jnp.concatenate([t[..., A_HEAD_DIM:], t[..., :A_HEAD_DIM]], axis=-1)
    w_a = jnp.concatenate([wq, wk, swap(wk), wv, swap(wv)], axis=1).astype(BF16)
    b_a = jnp.concatenate([bq, bk, swap(bk), bv, swap(bv)])[None]
    cos_a, sin_a = _swa_tables(pos)
    q, kk, vv = _swa_qkv(h, attn_norm[0][None], w_a, b_a, cos_a, sin_a, lp)
    sink_rows = jnp.repeat(a_sinks[0].astype(F32).reshape(A_KV_HEADS, 4, 2).transpose(0, 2, 1).reshape(4, 4), BLOCK, axis=1)[:, None, :]
    o = _swa_attn(q, kk, vv, sink_rows, batch, lp, s_real)
    wr, br = router_params(0)
    h1, route_t = _attn_out(o, a_w_o[0].astype(BF16), a_b_o[0][None], h, ffn_norm[0][None], wr, br)
    h = moe(0, h1, route_t, False).reshape(tp, D_MODEL)

    scale = (B_NOPE + B_ROPE) ** -0.5
    w_in = b_w_in[0]
    half = B_ROPE // 2
    w_pe = w_in[:, B_Q_RANK + B_KV_RANK:]
    kpe_cols = jnp.pad(jnp.concatenate([w_pe, w_pe[:, :half]], axis=1), ((0, 0), (B_NOPE, 128 - B_NOPE - B_ROPE - half)))
    w_in_p = jnp.concatenate([w_in[:, :B_Q_RANK + B_KV_RANK], kpe_cols], axis=1).astype(BF16)
    w_uq = b_w_uq[0].reshape(B_Q_RANK, B_HEADS, B_NOPE + B_ROPE)
    w_uq = jnp.concatenate([w_uq, w_uq[:, :, B_NOPE:B_NOPE + half]], axis=2).reshape(B_Q_RANK, -1)
    w_q_p = _pad_heads(w_uq, B_HEADS, B_NOPE + B_ROPE + half, B_HEAD_PAD).astype(BF16)
    w_ukv = b_w_ukv[0].reshape(B_KV_RANK, B_HEADS, B_NOPE + B_V)
    w_k_p = _pad_heads(w_ukv[:, :, :B_NOPE].reshape(B_KV_RANK, -1), B_HEADS, B_NOPE, B_HEAD_PAD).astype(BF16)
    w_v_p = _pad_heads(w_ukv[:, :, B_NOPE:].reshape(B_KV_RANK, -1), B_HEADS, B_V, B_HEAD_PAD).astype(BF16)
    tabs = _mla_tables(pos, scale * math.log2(math.e)) + _mla_tables(pos, 1.0)
    q, k, v = _mla_proj(h, attn_norm[1][None], w_in_p, b_q_norm[0][None], b_kv_norm[0][None],
                        w_q_p, w_k_p, w_v_p, tabs, batch, lp)
    o = _mla_attn(q, k, v, batch, lp, s_real).reshape(tp, B_HEADS * B_HEAD_PAD)
    w_o_p = jnp.pad(b_w_o[0].reshape(B_HEADS, B_V, D_MODEL), ((0, 0), (0, B_HEAD_PAD - B_V), (0, 0)))
    w_o_p = w_o_p.reshape(B_HEADS * B_HEAD_PAD, D_MODEL).astype(BF16)
    wr, br = router_params(1)
    h1, route_t = _attn_out(o, w_o_p, jnp.zeros((1, D_MODEL), F32), h, ffn_norm[1][None], wr, br)
    return moe(1, h1, route_t, True)
```
